```python
import math
import jax, jax.numpy as jnp
from jax import lax
import numpy as np

D_MODEL = 1024
BATCH = 8
SEQ = 4096
DEPTH = 2

CHUNK = 64
SSD_HEADS = 16
SSD_HEAD_DIM = 64
SSD_INNER = SSD_HEADS * SSD_HEAD_DIM
SSD_GROUPS = 2
SSD_STATE = 128
SSD_CONV = 4
SSD_CHUNK = CHUNK
SSD_CONV_DIM = SSD_INNER + 2 * SSD_GROUPS * SSD_STATE
SGU_BLOCK = 128
SGU_GROUPS = 8
SGU_WIDTH = 1024
SGU_GROUP_DIM = SGU_WIDTH // SGU_GROUPS
EVEN_IN = SSD_INNER + SSD_CONV_DIM + SSD_HEADS + 2 * SGU_WIDTH
EVEN_MIX = SSD_INNER + SGU_WIDTH
DIFF_HEADS = 8
DIFF_HEAD_DIM = 64
DIFF_V_DIM = 2 * DIFF_HEAD_DIM
DIFF_QK = DIFF_HEADS * 2 * DIFF_HEAD_DIM
Q_BLOCK = 128
D_FF = 4 * D_MODEL
DEEPNORM_ALPHA = (2 * DEPTH) ** 0.25
DEEPNORM_BETA = (8 * DEPTH) ** -0.25
N_EVEN = (DEPTH + 1) // 2
N_ODD = DEPTH // 2
LN_EPS = 1e-5

kernel_name = 'hybrid_ssd_sgu_diffattn_deepnorm'


def layer_norm(x, g, b):
    xf = x.astype(jnp.float32)
    mu = jnp.mean(xf, axis=-1, keepdims=True)
    var = jnp.mean(jnp.square(xf - mu), axis=-1, keepdims=True)
    y = (xf - mu) * lax.rsqrt(var + LN_EPS) * g.astype(jnp.float32) + b.astype(jnp.float32)
    return y.astype(x.dtype)


def rms_norm(x, w):
    xf = x.astype(jnp.float32)
    y = xf * lax.rsqrt(jnp.mean(jnp.square(xf), axis=-1, keepdims=True) + LN_EPS)
    return (y * w.astype(jnp.float32)).astype(x.dtype)


def causal_dwconv(x, w, b):
    c = x.shape[-1]
    y = lax.conv_general_dilated(x, w[:, None, :].astype(x.dtype), window_strides=(1,),
                                 padding=[(SSD_CONV - 1, 0)],
                                 dimension_numbers=('NWC', 'WIO', 'NWC'),
                                 feature_group_count=c)
    return y + b.astype(x.dtype)


def ssd_chunked_scan(x, dt, a, bm, cm):
    bsz, s, h, p = x.shape
    L = SSD_CHUNK
    nc = s // L
    g = SSD_GROUPS
    hg = h // g
    n = bm.shape[-1]
    xdt = (x * dt[..., None]).reshape(bsz, nc, L, g, hg, p)
    da = (dt * a).reshape(bsz, nc, L, g, hg)
    bm = bm.reshape(bsz, nc, L, g, n)
    cm = cm.reshape(bsz, nc, L, g, n)
    a_cs = jnp.cumsum(da, axis=2)
    causal = jnp.tril(jnp.ones((L, L), dtype=bool))
    seg = a_cs[:, :, :, None] - a_cs[:, :, None, :]
    decay = jnp.exp(jnp.where(causal[None, None, :, :, None, None], seg, -jnp.inf))
    cb = jnp.einsum('bctgn,bcsgn->bctsg', cm, bm)
    y_diag = jnp.einsum('bctsgh,bcsghp->bctghp', cb[..., None] * decay, xdt)
    state_decay = jnp.exp(a_cs[:, :, -1:] - a_cs)
    chunk_states = jnp.einsum('bclgn,bclgh,bclghp->bcghpn', bm, state_decay, xdt)
    chunk_decay = jnp.exp(a_cs[:, :, -1])

    def step(state, inp):
        s_c, d_c = inp
        return d_c[..., None, None] * state + s_c, state

    h0 = jnp.zeros((bsz, g, hg, p, n), jnp.float32)
    _, prev = lax.scan(step, h0, (jnp.moveaxis(chunk_states, 1, 0), jnp.moveaxis(chunk_decay, 1, 0)))
    prev = jnp.moveaxis(prev, 0, 1)
    y_off = jnp.einsum('bctgn,bcghpn,bctgh->bctghp', cm, prev, jnp.exp(a_cs))
    return (y_diag + y_off).reshape(bsz, s, h, p)


def ssd_mixer(z, xbc, dt_raw, conv_w, conv_b, dt_bias, a_log, d_skip, norm_w):
    bsz, s, _ = z.shape
    f32 = jnp.float32
    xbc = jax.nn.silu(causal_dwconv(xbc, conv_w, conv_b))
    xs, bm, cm = jnp.split(xbc, [SSD_INNER, SSD_INNER + SSD_GROUPS * SSD_STATE], axis=-1)
    xs = xs.astype(f32).reshape(bsz, s, SSD_HEADS, SSD_HEAD_DIM)
    bm = bm.astype(f32).reshape(bsz, s, SSD_GROUPS, SSD_STATE)
    cm = cm.astype(f32).reshape(bsz, s, SSD_GROUPS, SSD_STATE)
    dt = jax.nn.softplus(dt_raw.astype(f32) + dt_bias.astype(f32))
    a = -jnp.exp(a_log.astype(f32))
    y = ssd_chunked_scan(xs, dt, a, bm, cm) + xs * d_skip.astype(f32)[:, None]
    y = y.reshape(bsz, s, SSD_INNER) * jax.nn.silu(z.astype(f32))
    yg = y.reshape(bsz, s, SSD_GROUPS, SSD_INNER // SSD_GROUPS)
    yg = yg * lax.rsqrt(jnp.mean(jnp.square(yg), axis=-1, keepdims=True) + LN_EPS)
    y = yg.reshape(bsz, s, SSD_INNER) * norm_w.astype(f32)
    return y.astype(z.dtype)


def sgu_mixer(proj, ln_g, ln_b, w_s, b_s):
    bsz, s, _ = proj.shape
    u, v = jnp.split(jax.nn.gelu(proj, approximate=False), 2, axis=-1)
    v = layer_norm(v, ln_g, ln_b)
    nb = s // SGU_BLOCK
    v = v.reshape(bsz, nb, SGU_BLOCK, SGU_GROUPS, SGU_GROUP_DIM)
    cpos = jnp.arange(SGU_BLOCK) // CHUNK
    mask = cpos[:, None] >= cpos[None, :]
    w = jnp.where(mask[None], w_s, jnp.zeros_like(w_s))
    mixed = jnp.einsum('gts,bnsgc->bntgc', w, v) + b_s.T[None, None, :, :, None]
    return u * mixed.reshape(bsz, s, SGU_WIDTH)


def alibi_slopes(n_heads):
    return jnp.exp2(-8.0 * jnp.arange(1, n_heads + 1, dtype=jnp.float32) / n_heads)


def diff_attention(x, w_qkv, lq1, lk1, lq2, lk2, subln_w, w_out, lambda_init):
    bsz, s, _ = x.shape
    f32 = jnp.float32
    qkv = x @ w_qkv
    q, k, v = jnp.split(qkv, [DIFF_QK, 2 * DIFF_QK], axis=-1)
    q = q.reshape(bsz, s, DIFF_HEADS, 2, DIFF_HEAD_DIM)
    k = k.reshape(bsz, s, DIFF_HEADS, 2, DIFF_HEAD_DIM)
    v = v.reshape(bsz, s, DIFF_HEADS, DIFF_V_DIM)
    lam = (jnp.exp(jnp.sum(lq1.astype(f32) * lk1.astype(f32)))
           - jnp.exp(jnp.sum(lq2.astype(f32) * lk2.astype(f32))) + lambda_init)
    slopes = alibi_slopes(DIFF_HEADS)
    nb = s // Q_BLOCK
    qb = jnp.moveaxis(q.reshape(bsz, nb, Q_BLOCK, DIFF_HEADS, 2, DIFF_HEAD_DIM), 1, 0)
    k_pos = jnp.arange(s)
    scale = DIFF_HEAD_DIM ** -0.5

    def block(args):
        q_i, i = args
        q_pos = i * Q_BLOCK + jnp.arange(Q_BLOCK)
        sc = jnp.einsum('bqhmd,bkhmd->bhmqk', q_i, k, preferred_element_type=f32) * scale
        dist = jnp.abs(q_pos[:, None] - k_pos[None, :]).astype(f32)
        allowed = (k_pos[None, :] // CHUNK) <= (q_pos[:, None] // CHUNK)
        bias = jnp.where(allowed[None], -slopes[:, None, None] * dist[None], -jnp.inf)
        pr = jax.nn.softmax(sc + bias[None, :, None], axis=-1)
        attn = pr[:, :, 0] - lam * pr[:, :, 1]
        return jnp.einsum('bhqk,bkhe->bqhe', attn.astype(v.dtype), v)

    o = lax.map(block, (qb, jnp.arange(nb)))
    o = jnp.moveaxis(o, 0, 1).reshape(bsz, s, DIFF_HEADS, DIFF_V_DIM)
    o = rms_norm(o, subln_w) * (1.0 - lambda_init)
    return o.reshape(bsz, s, DIFF_HEADS * DIFF_V_DIM) @ w_out


def setup_inputs(seed: int = 0) -> dict:
    key = jax.random.key(seed)
    ks = jax.random.split(key, 28)
    f32 = jnp.float32

    def nrm(k, shape, sd):
        return sd * jax.random.normal(k, shape, f32)

    x = jax.random.normal(ks[0], (BATCH, SEQ, D_MODEL), f32)
    even_w_in = nrm(ks[1], (N_EVEN, D_MODEL, EVEN_IN), D_MODEL ** -0.5)
    even_conv_w = nrm(ks[2], (N_EVEN, SSD_CONV, SSD_CONV_DIM), SSD_CONV ** -0.5)
    even_conv_b = nrm(ks[3], (N_EVEN, SSD_CONV_DIM), 0.01)
    dt0 = jnp.exp(jax.random.uniform(ks[4], (N_EVEN, SSD_HEADS), f32, math.log(1e-3), math.log(1e-1)))
    even_dt_bias = dt0 + jnp.log(-jnp.expm1(-dt0))
    even_a_log = jnp.log(jax.random.uniform(ks[5], (N_EVEN, SSD_HEADS), f32, 1.0, 16.0))
    even_d = 1.0 + nrm(ks[6], (N_EVEN, SSD_HEADS), 0.1)
    even_ssd_norm_w = 1.0 + nrm(ks[7], (N_EVEN, SSD_INNER), 0.02)
    even_sgu_ln_g = 1.0 + nrm(ks[8], (N_EVEN, SGU_WIDTH), 0.02)
    even_sgu_ln_b = nrm(ks[9], (N_EVEN, SGU_WIDTH), 0.02)
    even_sgu_w = nrm(ks[10], (N_EVEN, SGU_GROUPS, SGU_BLOCK, SGU_BLOCK), 0.5 * SGU_BLOCK ** -0.5)
    even_sgu_b = 1.0 + nrm(ks[11], (N_EVEN, SGU_GROUPS, SGU_BLOCK), 0.02)
    even_w_out = nrm(ks[12], (N_EVEN, EVEN_MIX, D_MODEL), EVEN_MIX ** -0.5 * DEEPNORM_BETA)
    w_qk = nrm(ks[13], (N_ODD, D_MODEL, 2 * DIFF_QK), D_MODEL ** -0.5)
    w_v = nrm(ks[14], (N_ODD, D_MODEL, DIFF_HEADS * DIFF_V_DIM), D_MODEL ** -0.5 * DEEPNORM_BETA)
    odd_w_qkv = jnp.concatenate([w_qk, w_v], axis=-1)
    odd_lambda_q1 = nrm(ks[15], (N_ODD, DIFF_HEAD_DIM), 0.1)
    odd_lambda_k1 = nrm(ks[16], (N_ODD, DIFF_HEAD_DIM), 0.1)
    odd_lambda_q2 = nrm(ks[17], (N_ODD, DIFF_HEAD_DIM), 0.1)
    odd_lambda_k2 = nrm(ks[18], (N_ODD, DIFF_HEAD_DIM), 0.1)
    odd_subln_w = 1.0 + nrm(ks[19], (N_ODD, DIFF_V_DIM), 0.02)
    odd_w_out = nrm(ks[20], (N_ODD, DIFF_HEADS * DIFF_V_DIM, D_MODEL),
                    (DIFF_HEADS * DIFF_V_DIM) ** -0.5 * DEEPNORM_BETA)
    ln_mix_g = 1.0 + nrm(ks[21], (DEPTH, D_MODEL), 0.02)
    ln_mix_b = nrm(ks[22], (DEPTH, D_MODEL), 0.02)
    ln_ffn_g = 1.0 + nrm(ks[23], (DEPTH, D_MODEL), 0.02)
    ln_ffn_b = nrm(ks[24], (DEPTH, D_MODEL), 0.02)
    mlp_w_up = nrm(ks[25], (DEPTH, D_MODEL, D_FF), D_MODEL ** -0.5 * DEEPNORM_BETA)
    mlp_w_down = nrm(ks[26], (DEPTH, D_FF, D_MODEL), D_FF ** -0.5 * DEEPNORM_BETA)
    return {'x': x, 'even_w_in': even_w_in, 'even_conv_w': even_conv_w, 'even_conv_b': even_conv_b,
            'even_dt_bias': even_dt_bias, 'even_a_log': even_a_log, 'even_d': even_d,
            'even_ssd_norm_w': even_ssd_norm_w, 'even_sgu_ln_g': even_sgu_ln_g,
            'even_sgu_ln_b': even_sgu_ln_b, 'even_sgu_w': even_sgu_w, 'even_sgu_b': even_sgu_b,
            'even_w_out': even_w_out, 'odd_w_qkv': odd_w_qkv, 'odd_lambda_q1': odd_lambda_q1,
            'odd_lambda_k1': odd_lambda_k1, 'odd_lambda_q2': odd_lambda_q2, 'odd_lambda_k2': odd_lambda_k2,
            'odd_subln_w': odd_subln_w, 'odd_w_out': odd_w_out, 'ln_mix_g': ln_mix_g, 'ln_mix_b': ln_mix_b,
            'ln_ffn_g': ln_ffn_g, 'ln_ffn_b': ln_ffn_b, 'mlp_w_up': mlp_w_up, 'mlp_w_down': mlp_w_down}


def reference(x, even_w_in, even_conv_w, even_conv_b, even_dt_bias, even_a_log, even_d,
              even_ssd_norm_w, even_sgu_ln_g, even_sgu_ln_b, even_sgu_w, even_sgu_b, even_w_out,
              odd_w_qkv, odd_lambda_q1, odd_lambda_k1, odd_lambda_q2, odd_lambda_k2, odd_subln_w,
              odd_w_out, ln_mix_g, ln_mix_b, ln_ffn_g, ln_ffn_b, mlp_w_up, mlp_w_down):
    h = x
    split_pts = [SSD_INNER, SSD_INNER + SSD_CONV_DIM, SSD_INNER + SSD_CONV_DIM + SSD_HEADS]
    for l in range(DEPTH):
        i = l // 2
        if l % 2 == 0:
            proj = h @ even_w_in[i]
            z, xbc, dt_raw, sgu_in = jnp.split(proj, split_pts, axis=-1)
            y_a = ssd_mixer(z, xbc, dt_raw, even_conv_w[i], even_conv_b[i], even_dt_bias[i],
                            even_a_log[i], even_d[i], even_ssd_norm_w[i])
            y_b = sgu_mixer(sgu_in, even_sgu_ln_g[i], even_sgu_ln_b[i], even_sgu_w[i], even_sgu_b[i])
            mix = jnp.concatenate([y_a, y_b], axis=-1) @ even_w_out[i]
        else:
            lambda_init = 0.8 - 0.6 * math.exp(-0.3 * l)
            mix = diff_attention(h, odd_w_qkv[i], odd_lambda_q1[i], odd_lambda_k1[i],
                                 odd_lambda_q2[i], odd_lambda_k2[i], odd_subln_w[i], odd_w_out[i],
                                 lambda_init)
        h = layer_norm(DEEPNORM_ALPHA * h + mix, ln_mix_g[l], ln_mix_b[l])
        f = jnp.square(jax.nn.relu(h @ mlp_w_up[l])) @ mlp_w_down[l]
        h = layer_norm(DEEPNORM_ALPHA * h + f, ln_ffn_g[l], ln_ffn_b[l])
    return h
```

```python
import functools
import math

import jax
import jax.numpy as jnp
from jax import lax
from jax.experimental import pallas as pl
from jax.experimental.pallas import tpu as pltpu

F32 = jnp.float32
BF16 = jnp.bfloat16

D_MODEL = 1024
DEPTH = 2
CHUNK = 64
SSD_HEADS = 16
SSD_HEAD_DIM = 64
SSD_INNER = SSD_HEADS * SSD_HEAD_DIM
SSD_GROUPS = 2
SSD_STATE = 128
SSD_CONV = 4
SSD_BC = SSD_GROUPS * SSD_STATE
SSD_CONV_DIM = SSD_INNER + 2 * SSD_BC
SSD_GROUP_WIDTH = SSD_INNER // SSD_GROUPS
SGU_BLOCK = 128
SGU_GROUPS = 8
SGU_WIDTH = 1024
SGU_GROUP_DIM = SGU_WIDTH // SGU_GROUPS
DIFF_HEADS = 8
DIFF_HEAD_DIM = 64
DIFF_V_DIM = 2 * DIFF_HEAD_DIM
DIFF_QK = DIFF_HEADS * 2 * DIFF_HEAD_DIM
D_FF = 4 * D_MODEL
DEEPNORM_ALPHA = (2 * DEPTH) ** 0.25
LN_EPS = 1e-5

V7X_LANES = 128
V7X_SUBLANES = 8
V7X_VMEM_BYTES = 64 * 1024 * 1024
VMEM_LIMIT = 56 * 1024 * 1024

PROJ_TM = 1024
PROJ_TN = 1536
LN_TM = 512
MLP_TM = 512
MLP_FF_CHUNK = 1024
SSD_L = 256
SGU_TM = 512
ATT_TQ = 256
ATT_TK = 256
DT_PAD = V7X_LANES


def _params(*sem):
    return pltpu.CompilerParams(dimension_semantics=sem, vmem_limit_bytes=VMEM_LIMIT)


def _layer_norm(y, g, b):
    mu = jnp.mean(y, axis=-1, keepdims=True)
    yc = y - mu
    var = jnp.mean(yc * yc, axis=-1, keepdims=True)
    return yc * lax.rsqrt(var + LN_EPS) * g + b


def _dot(a, b):
    return jnp.dot(a, b, preferred_element_type=F32)


def _dot_nt(a, b):
    return lax.dot_general(a, b, (((1,), (1,)), ((), ())), preferred_element_type=F32)


def _matmul_kernel(x_ref, w_ref, o_ref):
    o_ref[...] = _dot(x_ref[...].astype(BF16), w_ref[...]).astype(o_ref.dtype)


def _matmul_dt_kernel(x_ref, w_ref, wdt_ref, o_ref, dt_ref):
    xb = x_ref[...].astype(BF16)
    o_ref[...] = _dot(xb, w_ref[...]).astype(o_ref.dtype)

    @pl.when(pl.program_id(1) == 0)
    def _():
        dt_ref[...] = _dot(xb, wdt_ref[...])


def _matmul(x, w, tm, tn, out_dtype, name, w_dt=None):
    t, k = x.shape
    n = w.shape[1]
    grid = (t // tm, n // tn)
    x_spec = pl.BlockSpec((tm, k), lambda i, j: (i, 0))
    w_spec = pl.BlockSpec((k, tn), lambda i, j: (0, j))
    o_spec = pl.BlockSpec((tm, tn), lambda i, j: (i, j))
    if w_dt is None:
        return pl.pallas_call(
            _matmul_kernel, grid=grid, in_specs=[x_spec, w_spec], out_specs=o_spec,
            out_shape=jax.ShapeDtypeStruct((t, n), out_dtype),
            compiler_params=_params("parallel", "arbitrary"), name=name)(x, w)
    return pl.pallas_call(
        _matmul_dt_kernel, grid=grid,
        in_specs=[x_spec, w_spec, pl.BlockSpec((k, DT_PAD), lambda i, j: (0, 0))],
        out_specs=[o_spec, pl.BlockSpec((tm, DT_PAD), lambda i, j: (i, 0))],
        out_shape=[jax.ShapeDtypeStruct((t, n), out_dtype), jax.ShapeDtypeStruct((t, DT_PAD), F32)],
        compiler_params=_params("parallel", "arbitrary"), name=name)(x, w, w_dt)


def _proj_ln_kernel(*refs, n_in):
    x_refs, w_refs = refs[:n_in], refs[n_in:2 * n_in]
    h_ref, g_ref, b_ref, o_ref = refs[2 * n_in:]
    acc = _dot(x_refs[0][...], w_refs[0][...])
    for x_ref, w_ref in zip(x_refs[1:], w_refs[1:]):
        acc += _dot(x_ref[...], w_ref[...])
    o_ref[...] = _layer_norm(DEEPNORM_ALPHA * h_ref[...] + acc, g_ref[...], b_ref[...])


def _proj_ln(xs, ws, h, g, b, name):
    t, d = h.shape
    tm = LN_TM
    n_in = len(xs)
    in_specs = ([pl.BlockSpec((tm, x.shape[1]), lambda i: (i, 0)) for x in xs]
                + [pl.BlockSpec(w.shape, lambda i: (0, 0)) for w in ws]
                + [pl.BlockSpec((tm, d), lambda i: (i, 0)),
                   pl.BlockSpec((1, d), lambda i: (0, 0)), pl.BlockSpec((1, d), lambda i: (0, 0))])
    return pl.pallas_call(
        functools.partial(_proj_ln_kernel, n_in=n_in), grid=(t // tm,), in_specs=in_specs,
        out_specs=pl.BlockSpec((tm, d), lambda i: (i, 0)),
        out_shape=jax.ShapeDtypeStruct((t, d), F32),
        compiler_params=_params("parallel"), name=name)(*xs, *ws, h, g, b)


def _mlp_ln_kernel(h_ref, wu_ref, wd_ref, g_ref, b_ref, o_ref):
    h = h_ref[...]
    hb = h.astype(BF16)
    acc = jnp.zeros(h.shape, F32)
    for c in range(D_FF // MLP_FF_CHUNK):
        cols = slice(c * MLP_FF_CHUNK, (c + 1) * MLP_FF_CHUNK)
        u = jnp.maximum(_dot(hb, wu_ref[:, cols]), 0.0)
        acc += _dot((u * u).astype(BF16), wd_ref[cols, :])
    o_ref[...] = _layer_norm(DEEPNORM_ALPHA * h + acc, g_ref[...], b_ref[...])


def _mlp_ln(h, w_up, w_down, g, b, name):
    t, d = h.shape
    tm = MLP_TM
    return pl.pallas_call(
        _mlp_ln_kernel, grid=(t // tm,),
        in_specs=[pl.BlockSpec((tm, d), lambda i: (i, 0)),
                  pl.BlockSpec(w_up.shape, lambda i: (0, 0)),
                  pl.BlockSpec(w_down.shape, lambda i: (0, 0)),
                  pl.BlockSpec((1, d), lambda i: (0, 0)), pl.BlockSpec((1, d), lambda i: (0, 0))],
        out_specs=pl.BlockSpec((tm, d), lambda i: (i, 0)),
        out_shape=jax.ShapeDtypeStruct((t, d), F32),
        compiler_params=_params("parallel"), name=name)(h, w_up, w_down, g, b)


def _softplus(x):
    return jnp.maximum(x, 0.0) + jnp.log1p(jnp.exp(-jnp.abs(x)))


def _silu(x):
    return x * jax.nn.sigmoid(x)


def _cumsum_rows(x):
    n = x.shape[0]
    row = lax.broadcasted_iota(jnp.int32, x.shape, 0)
    shift = 1
    while shift < n:
        x = x + jnp.where(row >= shift, pltpu.roll(x, shift, 0), 0.0)
        shift *= 2
    return x


def _ssd_kernel(z_ref, xbc_ref, dtr_ref, cw_ref, cb_ref, dtb_ref, alog_ref, dskip_ref, nw_ref, e_ref,
                o_ref, ext_ref, state_ref):
    L = SSD_L
    pad = V7X_SUBLANES

    @pl.when(pl.program_id(1) == 0)
    def _():
        ext_ref[0:pad, :] = jnp.zeros((pad, SSD_CONV_DIM), F32)
        state_ref[...] = jnp.zeros(state_ref.shape, F32)

    ext_ref[pad:, :] = xbc_ref[...].astype(F32)
    conv = cb_ref[...] + cw_ref[0:1, :] * ext_ref[pad - 3:pad - 3 + L, :]
    for k in range(1, SSD_CONV):
        conv += cw_ref[k:k + 1, :] * ext_ref[pad - 3 + k:pad - 3 + k + L, :]
    ext_ref[0:pad, :] = ext_ref[L:L + pad, :]
    xbc = _silu(conv)
    xs = xbc[:, :SSD_INNER]
    bmat = xbc[:, SSD_INNER:SSD_INNER + SSD_BC]
    cmat = xbc[:, SSD_INNER + SSD_BC:]

    dt = _softplus(dtr_ref[...] + dtb_ref[...])
    a_cs = _cumsum_rows(dt * (-jnp.exp(alog_ref[...])))
    a_last = a_cs[L - 1:L, :]
    exp_a = jnp.exp(a_cs)
    to_end = jnp.exp(a_last - a_cs)
    stacked = jnp.concatenate([dt, exp_a, dt * to_end], axis=0)
    expanded = jnp.dot(stacked, e_ref[...], preferred_element_type=F32, precision=lax.Precision.HIGHEST)
    dt_e, exp_a_e, w_end_e = expanded[:L], expanded[L:2 * L], expanded[2 * L:]

    xdt_b = (xs * dt_e).astype(BF16)
    xend_b = (xs * w_end_e).astype(BF16)
    b_b = bmat.astype(BF16)
    c_b = cmat.astype(BF16)
    a_cs_t = a_cs.T
    causal = (lax.broadcasted_iota(jnp.int32, (L, L), 0) >= lax.broadcasted_iota(jnp.int32, (L, L), 1))
    lane = lax.broadcasted_iota(jnp.int32, (L, V7X_LANES), 1)
    heads_per_group = SSD_HEADS // SSD_GROUPS

    y_parts = []
    for g in range(SSD_GROUPS):
        gs = slice(g * SSD_STATE, (g + 1) * SSD_STATE)
        gw = slice(g * SSD_GROUP_WIDTH, (g + 1) * SSD_GROUP_WIDTH)
        cb = _dot_nt(c_b[:, gs], b_b[:, gs])
        y_off = _dot(c_b[:, gs], state_ref[g].astype(BF16)) * exp_a_e[:, gw]
        for pair in range(heads_per_group // 2):
            h0 = g * heads_per_group + 2 * pair
            cols = slice(h0 * SSD_HEAD_DIM, (h0 + 2) * SSD_HEAD_DIM)
            x_pair = xdt_b[:, cols]
            y_pair = None
            for sub in range(2):
                h = h0 + sub
                seg = a_cs[:, h:h + 1] - a_cs_t[h:h + 1, :]
                decay = jnp.exp(jnp.where(causal, seg, -jnp.inf))
                m = (cb * decay).astype(BF16)
                keep = (lane < SSD_HEAD_DIM) if sub == 0 else (lane >= SSD_HEAD_DIM)
                part = _dot(m, jnp.where(keep, x_pair, jnp.zeros_like(x_pair)))
                y_pair = part if y_pair is None else y_pair + part
            y_parts.append(y_pair + y_off[:, 2 * pair * SSD_HEAD_DIM:(2 * pair + 2) * SSD_HEAD_DIM])
        upd = lax.dot_general(b_b[:, gs], xend_b[:, gw], (((0,), (0,)), ((), ())), preferred_element_type=F32)
        state_ref[g] = state_ref[g] * exp_a_e[L - 1:L, gw] + upd

    y = jnp.concatenate(y_parts, axis=1) + xs * dskip_ref[...]
    y = y * _silu(z_ref[...].astype(F32))
    normed = []
    for g in range(SSD_GROUPS):
        yg = y[:, g * SSD_GROUP_WIDTH:(g + 1) * SSD_GROUP_WIDTH]
        normed.append(yg * lax.rsqrt(jnp.mean(yg * yg, axis=-1, keepdims=True) + LN_EPS))
    o_ref[...] = (jnp.concatenate(normed, axis=1) * nw_ref[...]).astype(o_ref.dtype)


def _ssd_mixer(proj, dt_raw, conv_w, conv_b, dt_bias, a_log, d_skip, norm_w, bsz, seq):
    steps = seq // SSD_L
    row = lambda b, s: b * steps + s
    pad16 = lambda v: jnp.pad(v.astype(F32), (0, DT_PAD - SSD_HEADS)).reshape(1, DT_PAD)
    head_of_channel = jnp.arange(SSD_INNER) // SSD_HEAD_DIM
    expand = (jnp.arange(DT_PAD)[:, None] == head_of_channel[None, :]).astype(F32)
    d_e = jnp.repeat(d_skip.astype(F32), SSD_HEAD_DIM).reshape(1, SSD_INNER)
    const = lambda shape: pl.BlockSpec(shape, lambda b, s: (0, 0))
    return pl.pallas_call(
        _ssd_kernel, grid=(bsz, steps),
        in_specs=[pl.BlockSpec((SSD_L, SSD_INNER), lambda b, s: (row(b, s), 0)),
                  pl.BlockSpec((SSD_L, SSD_CONV_DIM), lambda b, s: (row(b, s), 2)),
                  pl.BlockSpec((SSD_L, DT_PAD), lambda b, s: (row(b, s), 0)),
                  const((SSD_CONV, SSD_CONV_DIM)), const((1, SSD_CONV_DIM)),
                  const((1, DT_PAD)), const((1, DT_PAD)),
                  const((1, SSD_INNER)), const((1, SSD_INNER)), const((DT_PAD, SSD_INNER))],
        out_specs=pl.BlockSpec((SSD_L, SSD_INNER), lambda b, s: (row(b, s), 0)),
        out_shape=jax.ShapeDtypeStruct((bsz * seq, SSD_INNER), BF16),
        scratch_shapes=[pltpu.VMEM((SSD_L + V7X_SUBLANES, SSD_CONV_DIM), F32),
                        pltpu.VMEM((SSD_GROUPS, SSD_STATE, SSD_GROUP_WIDTH), F32)],
        compiler_params=_params("parallel", "arbitrary"), name="ssd_mixer",
    )(proj, proj, dt_raw, conv_w.astype(F32), conv_b.astype(F32).reshape(1, -1), pad16(dt_bias), pad16(a_log),
      d_e, norm_w.astype(F32).reshape(1, -1), expand)


def _gelu(x):
    return 0.5 * x * (1.0 + lax.erf(x * (1.0 / math.sqrt(2.0))))


def _sgu_kernel(u_ref, v_ref, g_ref, b_ref, w_ref, bs_ref, o_ref):
    v = _layer_norm(_gelu(v_ref[...].astype(F32)), g_ref[...], b_ref[...]).astype(BF16)
    r = lax.broadcasted_iota(jnp.int32, (SGU_BLOCK, SGU_BLOCK), 0) // CHUNK
    c = lax.broadcasted_iota(jnp.int32, (SGU_BLOCK, SGU_BLOCK), 1) // CHUNK
    for g in range(SGU_GROUPS):
        w = jnp.where(r >= c, w_ref[g], 0.0).astype(BF16)
        cols = slice(g * SGU_GROUP_DIM, (g + 1) * SGU_GROUP_DIM)
        for blk in range(SGU_TM // SGU_BLOCK):
            rows = slice(blk * SGU_BLOCK, (blk + 1) * SGU_BLOCK)
            mixed = _dot(w, v[rows, cols]) + bs_ref[g]
            o_ref[rows, cols] = (_gelu(u_ref[rows, cols].astype(F32)) * mixed).astype(o_ref.dtype)


def _sgu_mixer(proj, ln_g, ln_b, w_s, b_s):
    t = proj.shape[0]
    bias = jnp.broadcast_to(b_s.astype(F32)[:, :, None], (SGU_GROUPS, SGU_BLOCK, SGU_GROUP_DIM))
    return pl.pallas_call(
        _sgu_kernel, grid=(t // SGU_TM,),
        in_specs=[pl.BlockSpec((SGU_TM, SGU_WIDTH), lambda i: (i, 1)),
                  pl.BlockSpec((SGU_TM, SGU_WIDTH), lambda i: (i, 2)),
                  pl.BlockSpec((1, SGU_WIDTH), lambda i: (0, 0)), pl.BlockSpec((1, SGU_WIDTH), lambda i: (0, 0)),
                  pl.BlockSpec((SGU_GROUPS, SGU_BLOCK, SGU_BLOCK), lambda i: (0, 0, 0)),
                  pl.BlockSpec((SGU_GROUPS, SGU_BLOCK, SGU_GROUP_DIM), lambda i: (0, 0, 0))],
        out_specs=pl.BlockSpec((SGU_TM, SGU_WIDTH), lambda i: (i, 0)),
        out_shape=jax.ShapeDtypeStruct((t, SGU_WIDTH), BF16),
        compiler_params=_params("parallel"), name="sgu_mixer",
    )(proj, proj, ln_g.astype(F32).reshape(1, -1), ln_b.astype(F32).reshape(1, -1), w_s.astype(F32), bias)


def _attn_kernel(slopes_ref, q_ref, k_ref, v_ref, lq1_ref, lk1_ref, lq2_ref, lk2_ref, sw_ref, o_ref, *, lambda_init):
    tq, tk = ATT_TQ, ATT_TK
    i = pl.program_id(2)
    slope = slopes_ref[pl.program_id(1)]
    q = q_ref[...]
    lane = lax.broadcasted_iota(jnp.int32, q.shape, 1)
    zero = jnp.zeros_like(q)
    qq = jnp.concatenate([jnp.where(lane < DIFF_HEAD_DIM, q, zero), jnp.where(lane >= DIFF_HEAD_DIM, q, zero)], axis=0)
    row = lax.broadcasted_iota(jnp.int32, (2 * tq, tk), 0)
    row = jnp.where(row >= tq, row - tq, row)
    col = lax.broadcasted_iota(jnp.int32, (2 * tq, tk), 1)
    rel = row - col

    def step(j, carry, masked):
        m, l, acc = carry
        start = pl.multiple_of(j * tk, tk)
        k = k_ref[pl.ds(start, tk), :]
        v = v_ref[pl.ds(start, tk), :]
        dist = rel + (i * tq - j * tk)
        s = _dot_nt(qq, k) - slope * jnp.abs(dist).astype(F32)
        if masked:
            q_chunk = (row + i * tq) // CHUNK
            k_chunk = (col + j * tk) // CHUNK
            s = jnp.where(k_chunk <= q_chunk, s, -jnp.inf)
        m_new = jnp.maximum(m, jnp.max(s, axis=-1, keepdims=True))
        alpha = jnp.exp(m - m_new)
        p = jnp.exp(s - m_new)
        l = alpha * l + jnp.sum(p, axis=-1, keepdims=True)
        acc = alpha * acc + _dot(p.astype(BF16), v)
        return m_new, l, acc

    carry = (jnp.full((2 * tq, 1), -jnp.inf, F32), jnp.zeros((2 * tq, 1), F32), jnp.zeros((2 * tq, DIFF_V_DIM), F32))
    carry = lax.fori_loop(0, i, functools.partial(step, masked=False), carry)
    _, l, acc = step(i, carry, masked=True)

    o = acc / l
    lam = (jnp.exp(jnp.sum(lq1_ref[...] * lk1_ref[...], axis=-1, keepdims=True))
           - jnp.exp(jnp.sum(lq2_ref[...] * lk2_ref[...], axis=-1, keepdims=True)) + lambda_init)
    o = o[:tq] - lam * o[tq:]
    o = o * lax.rsqrt(jnp.mean(o * o, axis=-1, keepdims=True) + LN_EPS) * sw_ref[...]
    o_ref[...] = (o * (1.0 - lambda_init)).astype(o_ref.dtype)


def _diff_attention(qkv, lq1, lk1, lq2, lk2, subln_w, lambda_init, bsz, seq):
    slopes = jnp.exp2(-8.0 * jnp.arange(1, DIFF_HEADS + 1, dtype=F32) / DIFF_HEADS)
    vec = lambda p: p.astype(F32).reshape(1, -1)
    head_blk = 2 * DIFF_HEAD_DIM
    small = lambda n: pl.BlockSpec((1, n), lambda b, h, i: (0, 0))
    return pl.pallas_call(
        functools.partial(_attn_kernel, lambda_init=lambda_init),
        grid=(bsz, DIFF_HEADS, seq // ATT_TQ),
        in_specs=[pl.BlockSpec(memory_space=pltpu.SMEM),
                  pl.BlockSpec((None, ATT_TQ, head_blk), lambda b, h, i: (b, i, h)),
                  pl.BlockSpec((None, seq, head_blk), lambda b, h, i: (b, 0, DIFF_HEADS + h)),
                  pl.BlockSpec((None, seq, DIFF_V_DIM), lambda b, h, i: (b, 0, 2 * DIFF_HEADS + h)),
                  small(DIFF_HEAD_DIM), small(DIFF_HEAD_DIM), small(DIFF_HEAD_DIM), small(DIFF_HEAD_DIM),
                  small(DIFF_V_DIM)],
        out_specs=pl.BlockSpec((None, ATT_TQ, DIFF_V_DIM), lambda b, h, i: (b, i, h)),
        out_shape=jax.ShapeDtypeStruct((bsz, seq, DIFF_HEADS * DIFF_V_DIM), BF16),
        compiler_params=_params("parallel", "parallel", "arbitrary"), name="diff_attention",
    )(slopes, qkv, qkv, qkv, vec(lq1), vec(lk1), vec(lq2), vec(lk2), vec(subln_w))


def kernel(x, even_w_in, even_conv_w, even_conv_b, even_dt_bias, even_a_log, even_d, even_ssd_norm_w, even_sgu_ln_g, even_sgu_ln_b, even_sgu_w, even_sgu_b, even_w_out, odd_w_qkv, odd_lambda_q1, odd_lambda_k1, odd_lambda_q2, odd_lambda_k2, odd_subln_w, odd_w_out, ln_mix_g, ln_mix_b, ln_ffn_g, ln_ffn_b, mlp_w_up, mlp_w_down):
    bsz, seq, d = x.shape
    h = x.reshape(bsz * seq, d)
    row = lambda p: p.astype(F32).reshape(1, -1)
    for l in range(DEPTH):
        i = l // 2
        if l % 2 == 0:
            w_in = even_w_in[i]
            o_z, o_xbc, o_dt, o_sgu = 0, SSD_INNER, SSD_INNER + SSD_CONV_DIM, SSD_INNER + SSD_CONV_DIM + SSD_HEADS
            w_main = jnp.concatenate([w_in[:, o_z:o_xbc], w_in[:, o_sgu:], w_in[:, o_xbc:o_dt]], axis=1).astype(BF16)
            w_dt = jnp.pad(w_in[:, o_dt:o_sgu], ((0, 0), (0, DT_PAD - SSD_HEADS))).astype(BF16)
            proj, dt_raw = _matmul(h, w_main, PROJ_TM, PROJ_TN, BF16, "in_proj", w_dt=w_dt)
            y_a = _ssd_mixer(proj, dt_raw, even_conv_w[i], even_conv_b[i], even_dt_bias[i], even_a_log[i],
                             even_d[i], even_ssd_norm_w[i], bsz, seq)
            y_b = _sgu_mixer(proj, even_sgu_ln_g[i], even_sgu_ln_b[i], even_sgu_w[i], even_sgu_b[i])
            w_out = even_w_out[i].astype(BF16)
            h = _proj_ln([y_a, y_b], [w_out[:SSD_INNER], w_out[SSD_INNER:]], h, row(ln_mix_g[l]), row(ln_mix_b[l]),
                         "even_out_proj_ln")
        else:
            lambda_init = 0.8 - 0.6 * math.exp(-0.3 * l)
            w_qkv = odd_w_qkv[i]
            w_qkv = jnp.concatenate([w_qkv[:, :DIFF_QK] * (DIFF_HEAD_DIM ** -0.5), w_qkv[:, DIFF_QK:]], axis=1).astype(BF16)
            qkv = _matmul(h, w_qkv, PROJ_TM, PROJ_TN, BF16, "qkv_proj").reshape(bsz, seq, -1)
            o = _diff_attention(qkv, odd_lambda_q1[i], odd_lambda_k1[i], odd_lambda_q2[i], odd_lambda_k2[i],
                                odd_subln_w[i], lambda_init, bsz, seq)
            h = _proj_ln([o.reshape(bsz * seq, -1)], [odd_w_out[i].astype(BF16)], h, row(ln_mix_g[l]), row(ln_mix_b[l]),
                         "odd_out_proj_ln")
        h = _mlp_ln(h, mlp_w_up[l].astype(BF16), mlp_w_down[l].astype(BF16), row(ln_ffn_g[l]), row(ln_ffn_b[l]),
                    "mlp_ln_%d" % l)
    return h.reshape(bsz, seq, d)
```

```python
import functools
import math

import jax
import jax.numpy as jnp
from jax import lax
from jax.experimental import pallas as pl
from jax.experimental.pallas import tpu as pltpu

F32 = jnp.float32
BF16 = jnp.bfloat16

D_MODEL = 1024
DEPTH = 2
CHUNK = 64
SSD_HEADS = 16
SSD_HEAD_DIM = 64
SSD_INNER = SSD_HEADS * SSD_HEAD_DIM
SSD_GROUPS = 2
SSD_STATE = 128
SSD_CONV = 4
SSD_BC = SSD_GROUPS * SSD_STATE
SSD_CONV_DIM = SSD_INNER + 2 * SSD_BC
SSD_GROUP_WIDTH = SSD_INNER // SSD_GROUPS
SGU_BLOCK = 128
SGU_GROUPS = 8
SGU_WIDTH = 1024
SGU_GROUP_DIM = SGU_WIDTH // SGU_GROUPS
DIFF_HEADS = 8
DIFF_HEAD_DIM = 64
DIFF_V_DIM = 2 * DIFF_HEAD_DIM
DIFF_QK = DIFF_HEADS * 2 * DIFF_HEAD_DIM
D_FF = 4 * D_MODEL
DEEPNORM_ALPHA = (2 * DEPTH) ** 0.25
LN_EPS = 1e-5

V7X_LANES = 128
V7X_SUBLANES = 8
V7X_VMEM_BYTES = 64 * 1024 * 1024
VMEM_LIMIT = 56 * 1024 * 1024

PROJ_TM = 1024
PROJ_TN = 1536
LN_TM = 512
MLP_TM = 512
MLP_FF_CHUNK = 1024
SSD_L = 256
SGU_TM = 512
ATT_T = 512
LOG2_E = 1.0 / math.log(2.0)
DT_PAD = V7X_LANES


def _params(*sem):
    return pltpu.CompilerParams(dimension_semantics=sem, vmem_limit_bytes=VMEM_LIMIT)


def _layer_norm(y, g, b):
    mu = jnp.mean(y, axis=-1, keepdims=True)
    yc = y - mu
    var = jnp.mean(yc * yc, axis=-1, keepdims=True)
    return yc * lax.rsqrt(var + LN_EPS) * g + b


def _dot(a, b):
    return jnp.dot(a, b, preferred_element_type=F32)


def _dot_nt(a, b):
    return lax.dot_general(a, b, (((1,), (1,)), ((), ())), preferred_element_type=F32)


def _matmul_kernel(x_ref, w_ref, o_ref):
    o_ref[...] = _dot(x_ref[...].astype(BF16), w_ref[...]).astype(o_ref.dtype)


def _matmul_dt_kernel(x_ref, w_ref, wdt_ref, o_ref, dt_ref):
    xb = x_ref[...].astype(BF16)
    o_ref[...] = _dot(xb, w_ref[...]).astype(o_ref.dtype)

    @pl.when(pl.program_id(1) == 0)
    def _():
        dt_ref[...] = _dot(xb, wdt_ref[...])


def _matmul(x, w, tm, tn, out_dtype, name, w_dt=None):
    t, k = x.shape
    n = w.shape[1]
    grid = (t // tm, n // tn)
    x_spec = pl.BlockSpec((tm, k), lambda i, j: (i, 0))
    w_spec = pl.BlockSpec((k, tn), lambda i, j: (0, j))
    o_spec = pl.BlockSpec((tm, tn), lambda i, j: (i, j))
    if w_dt is None:
        return pl.pallas_call(
            _matmul_kernel, grid=grid, in_specs=[x_spec, w_spec], out_specs=o_spec,
            out_shape=jax.ShapeDtypeStruct((t, n), out_dtype),
            compiler_params=_params("parallel", "arbitrary"), name=name)(x, w)
    return pl.pallas_call(
        _matmul_dt_kernel, grid=grid,
        in_specs=[x_spec, w_spec, pl.BlockSpec((k, DT_PAD), lambda i, j: (0, 0))],
        out_specs=[o_spec, pl.BlockSpec((tm, DT_PAD), lambda i, j: (i, 0))],
        out_shape=[jax.ShapeDtypeStruct((t, n), out_dtype), jax.ShapeDtypeStruct((t, DT_PAD), F32)],
        compiler_params=_params("parallel", "arbitrary"), name=name)(x, w, w_dt)


def _proj_ln_kernel(*refs, n_in):
    x_refs, w_refs = refs[:n_in], refs[n_in:2 * n_in]
    h_ref, g_ref, b_ref, o_ref = refs[2 * n_in:]
    acc = _dot(x_refs[0][...], w_refs[0][...])
    for x_ref, w_ref in zip(x_refs[1:], w_refs[1:]):
        acc += _dot(x_ref[...], w_ref[...])
    o_ref[...] = _layer_norm(DEEPNORM_ALPHA * h_ref[...] + acc, g_ref[...], b_ref[...])


def _proj_ln(xs, ws, h, g, b, name):
    t, d = h.shape
    tm = LN_TM
    n_in = len(xs)
    in_specs = ([pl.BlockSpec((tm, x.shape[1]), lambda i: (i, 0)) for x in xs]
                + [pl.BlockSpec(w.shape, lambda i: (0, 0)) for w in ws]
                + [pl.BlockSpec((tm, d), lambda i: (i, 0)),
                   pl.BlockSpec((1, d), lambda i: (0, 0)), pl.BlockSpec((1, d), lambda i: (0, 0))])
    return pl.pallas_call(
        functools.partial(_proj_ln_kernel, n_in=n_in), grid=(t // tm,), in_specs=in_specs,
        out_specs=pl.BlockSpec((tm, d), lambda i: (i, 0)),
        out_shape=jax.ShapeDtypeStruct((t, d), F32),
        compiler_params=_params("parallel"), name=name)(*xs, *ws, h, g, b)


def _mlp_ln_kernel(h_ref, wu_ref, wd_ref, g_ref, b_ref, o_ref):
    h = h_ref[...]
    hb = h.astype(BF16)
    acc = jnp.zeros(h.shape, F32)
    for c in range(D_FF // MLP_FF_CHUNK):
        cols = slice(c * MLP_FF_CHUNK, (c + 1) * MLP_FF_CHUNK)
        u = jnp.maximum(_dot(hb, wu_ref[:, cols]), 0.0)
        acc += _dot((u * u).astype(BF16), wd_ref[cols, :])
    o_ref[...] = _layer_norm(DEEPNORM_ALPHA * h + acc, g_ref[...], b_ref[...])


def _mlp_ln(h, w_up, w_down, g, b, name):
    t, d = h.shape
    tm = MLP_TM
    return pl.pallas_call(
        _mlp_ln_kernel, grid=(t // tm,),
        in_specs=[pl.BlockSpec((tm, d), lambda i: (i, 0)),
                  pl.BlockSpec(w_up.shape, lambda i: (0, 0)),
                  pl.BlockSpec(w_down.shape, lambda i: (0, 0)),
                  pl.BlockSpec((1, d), lambda i: (0, 0)), pl.BlockSpec((1, d), lambda i: (0, 0))],
        out_specs=pl.BlockSpec((tm, d), lambda i: (i, 0)),
        out_shape=jax.ShapeDtypeStruct((t, d), F32),
        compiler_params=_params("parallel"), name=name)(h, w_up, w_down, g, b)


def _softplus(x):
    return jnp.maximum(x, 0.0) + jnp.log1p(jnp.exp(-jnp.abs(x)))


def _silu(x):
    return x * jax.nn.sigmoid(x)


def _cumsum_rows(x):
    n = x.shape[0]
    row = lax.broadcasted_iota(jnp.int32, x.shape, 0)
    shift = 1
    while shift < n:
        x = x + jnp.where(row >= shift, pltpu.roll(x, shift, 0), 0.0)
        shift *= 2
    return x


def _ssd_kernel(z_ref, xbc_ref, dtr_ref, cw_ref, cb_ref, dtb_ref, alog_ref, dskip_ref, nw_ref, e_ref,
                o_ref, ext_ref, state_ref):
    L = SSD_L
    pad = V7X_SUBLANES

    @pl.when(pl.program_id(1) == 0)
    def _():
        ext_ref[0:pad, :] = jnp.zeros((pad, SSD_CONV_DIM), F32)
        state_ref[...] = jnp.zeros(state_ref.shape, F32)

    ext_ref[pad:, :] = xbc_ref[...].astype(F32)
    conv = cb_ref[...] + cw_ref[0:1, :] * ext_ref[pad - 3:pad - 3 + L, :]
    for k in range(1, SSD_CONV):
        conv += cw_ref[k:k + 1, :] * ext_ref[pad - 3 + k:pad - 3 + k + L, :]
    ext_ref[0:pad, :] = ext_ref[L:L + pad, :]
    xbc = _silu(conv)
    xs = xbc[:, :SSD_INNER]
    bmat = xbc[:, SSD_INNER:SSD_INNER + SSD_BC]
    cmat = xbc[:, SSD_INNER + SSD_BC:]

    dt = _softplus(dtr_ref[...] + dtb_ref[...])
    a_cs = _cumsum_rows(dt * (-jnp.exp(alog_ref[...])))
    a_last = a_cs[L - 1:L, :]
    exp_a = jnp.exp(a_cs)
    to_end = jnp.exp(a_last - a_cs)
    stacked = jnp.concatenate([dt, exp_a, dt * to_end], axis=0)
    expanded = jnp.dot(stacked, e_ref[...], preferred_element_type=F32, precision=lax.Precision.HIGHEST)
    dt_e, exp_a_e, w_end_e = expanded[:L], expanded[L:2 * L], expanded[2 * L:]

    xdt_b = (xs * dt_e).astype(BF16)
    xend_b = (xs * w_end_e).astype(BF16)
    b_b = bmat.astype(BF16)
    c_b = cmat.astype(BF16)
    a_cs_t = a_cs.T
    causal = (lax.broadcasted_iota(jnp.int32, (L, L), 0) >= lax.broadcasted_iota(jnp.int32, (L, L), 1))
    lane = lax.broadcasted_iota(jnp.int32, (L, V7X_LANES), 1)
    heads_per_group = SSD_HEADS // SSD_GROUPS

    y_parts = []
    for g in range(SSD_GROUPS):
        gs = slice(g * SSD_STATE, (g + 1) * SSD_STATE)
        gw = slice(g * SSD_GROUP_WIDTH, (g + 1) * SSD_GROUP_WIDTH)
        cb = _dot_nt(c_b[:, gs], b_b[:, gs])
        y_off = _dot(c_b[:, gs], state_ref[g].astype(BF16)) * exp_a_e[:, gw]
        for pair in range(heads_per_group // 2):
            h0 = g * heads_per_group + 2 * pair
            cols = slice(h0 * SSD_HEAD_DIM, (h0 + 2) * SSD_HEAD_DIM)
            x_pair = xdt_b[:, cols]
            y_pair = None
            for sub in range(2):
                h = h0 + sub
                seg = a_cs[:, h:h + 1] - a_cs_t[h:h + 1, :]
                decay = jnp.exp(jnp.where(causal, seg, -jnp.inf))
                m = (cb * decay).astype(BF16)
                keep = (lane < SSD_HEAD_DIM) if sub == 0 else (lane >= SSD_HEAD_DIM)
                part = _dot(m, jnp.where(keep, x_pair, jnp.zeros_like(x_pair)))
                y_pair = part if y_pair is None else y_pair + part
            y_parts.append(y_pair + y_off[:, 2 * pair * SSD_HEAD_DIM:(2 * pair + 2) * SSD_HEAD_DIM])
        upd = lax.dot_general(b_b[:, gs], xend_b[:, gw], (((0,), (0,)), ((), ())), preferred_element_type=F32)
        state_ref[g] = state_ref[g] * exp_a_e[L - 1:L, gw] + upd

    y = jnp.concatenate(y_parts, axis=1) + xs * dskip_ref[...]
    y = y * _silu(z_ref[...].astype(F32))
    normed = []
    for g in range(SSD_GROUPS):
        yg = y[:, g * SSD_GROUP_WIDTH:(g + 1) * SSD_GROUP_WIDTH]
        normed.append(yg * lax.rsqrt(jnp.mean(yg * yg, axis=-1, keepdims=True) + LN_EPS))
    o_ref[...] = (jnp.concatenate(normed, axis=1) * nw_ref[...]).astype(o_ref.dtype)


def _ssd_mixer(proj, dt_raw, conv_w, conv_b, dt_bias, a_log, d_skip, norm_w, bsz, seq):
    steps = seq // SSD_L
    row = lambda b, s: b * steps + s
    pad16 = lambda v: jnp.pad(v.astype(F32), (0, DT_PAD - SSD_HEADS)).reshape(1, DT_PAD)
    head_of_channel = jnp.arange(SSD_INNER) // SSD_HEAD_DIM
    expand = (jnp.arange(DT_PAD)[:, None] == head_of_channel[None, :]).astype(F32)
    d_e = jnp.repeat(d_skip.astype(F32), SSD_HEAD_DIM).reshape(1, SSD_INNER)
    const = lambda shape: pl.BlockSpec(shape, lambda b, s: (0, 0))
    return pl.pallas_call(
        _ssd_kernel, grid=(bsz, steps),
        in_specs=[pl.BlockSpec((SSD_L, SSD_INNER), lambda b, s: (row(b, s), 0)),
                  pl.BlockSpec((SSD_L, SSD_CONV_DIM), lambda b, s: (row(b, s), 2)),
                  pl.BlockSpec((SSD_L, DT_PAD), lambda b, s: (row(b, s), 0)),
                  const((SSD_CONV, SSD_CONV_DIM)), const((1, SSD_CONV_DIM)),
                  const((1, DT_PAD)), const((1, DT_PAD)),
                  const((1, SSD_INNER)), const((1, SSD_INNER)), const((DT_PAD, SSD_INNER))],
        out_specs=pl.BlockSpec((SSD_L, SSD_INNER), lambda b, s: (row(b, s), 0)),
        out_shape=jax.ShapeDtypeStruct((bsz * seq, SSD_INNER), BF16),
        scratch_shapes=[pltpu.VMEM((SSD_L + V7X_SUBLANES, SSD_CONV_DIM), F32),
                        pltpu.VMEM((SSD_GROUPS, SSD_STATE, SSD_GROUP_WIDTH), F32)],
        compiler_params=_params("parallel", "arbitrary"), name="ssd_mixer",
    )(proj, proj, dt_raw, conv_w.astype(F32), conv_b.astype(F32).reshape(1, -1), pad16(dt_bias), pad16(a_log),
      d_e, norm_w.astype(F32).reshape(1, -1), expand)


def _gelu(x):
    return 0.5 * x * (1.0 + lax.erf(x * (1.0 / math.sqrt(2.0))))


def _sgu_kernel(u_ref, v_ref, g_ref, b_ref, w_ref, bs_ref, o_ref):
    v = _layer_norm(_gelu(v_ref[...].astype(F32)), g_ref[...], b_ref[...]).astype(BF16)
    r = lax.broadcasted_iota(jnp.int32, (SGU_BLOCK, SGU_BLOCK), 0) // CHUNK
    c = lax.broadcasted_iota(jnp.int32, (SGU_BLOCK, SGU_BLOCK), 1) // CHUNK
    for g in range(SGU_GROUPS):
        w = jnp.where(r >= c, w_ref[g], 0.0).astype(BF16)
        cols = slice(g * SGU_GROUP_DIM, (g + 1) * SGU_GROUP_DIM)
        for blk in range(SGU_TM // SGU_BLOCK):
            rows = slice(blk * SGU_BLOCK, (blk + 1) * SGU_BLOCK)
            mixed = _dot(w, v[rows, cols]) + bs_ref[g]
            o_ref[rows, cols] = (_gelu(u_ref[rows, cols].astype(F32)) * mixed).astype(o_ref.dtype)


def _sgu_mixer(proj, ln_g, ln_b, w_s, b_s):
    t = proj.shape[0]
    bias = jnp.broadcast_to(b_s.astype(F32)[:, :, None], (SGU_GROUPS, SGU_BLOCK, SGU_GROUP_DIM))
    return pl.pallas_call(
        _sgu_kernel, grid=(t // SGU_TM,),
        in_specs=[pl.BlockSpec((SGU_TM, SGU_WIDTH), lambda i: (i, 1)),
                  pl.BlockSpec((SGU_TM, SGU_WIDTH), lambda i: (i, 2)),
                  pl.BlockSpec((1, SGU_WIDTH), lambda i: (0, 0)), pl.BlockSpec((1, SGU_WIDTH), lambda i: (0, 0)),
                  pl.BlockSpec((SGU_GROUPS, SGU_BLOCK, SGU_BLOCK), lambda i: (0, 0, 0)),
                  pl.BlockSpec((SGU_GROUPS, SGU_BLOCK, SGU_GROUP_DIM), lambda i: (0, 0, 0))],
        out_specs=pl.BlockSpec((SGU_TM, SGU_WIDTH), lambda i: (i, 0)),
        out_shape=jax.ShapeDtypeStruct((t, SGU_WIDTH), BF16),
        compiler_params=_params("parallel"), name="sgu_mixer",
    )(proj, proj, ln_g.astype(F32).reshape(1, -1), ln_b.astype(F32).reshape(1, -1), w_s.astype(F32), bias)


def _attn_kernel(slopes_ref, q_ref, k_ref, v_ref, lq1_ref, lk1_ref, lq2_ref, lk2_ref, sw_ref, o_ref,
                 vt_ref, dbias_ref, *, lambda_init):
    t = ATT_T
    head, b, i = pl.program_id(0), pl.program_id(1), pl.program_id(2)
    slope = slopes_ref[head]

    @pl.when(i == 0)
    def _():
        for c in range(vt_ref.shape[0]):
            vt_ref[c] = v_ref[c * t:(c + 1) * t, :].astype(F32).T.astype(BF16)

    @pl.when((i == 0) & (b == 0))
    def _():
        key = lax.broadcasted_iota(jnp.int32, (t, t), 0)
        qry = lax.broadcasted_iota(jnp.int32, (t, t), 1)
        dbias_ref[...] = jnp.where(key // CHUNK <= qry // CHUNK,
                                   slope * jnp.minimum(key, 2 * qry - key).astype(F32), -jnp.inf)

    q_t = q_ref[...].astype(F32).T
    feat = lax.broadcasted_iota(jnp.int32, q_t.shape, 0)
    qq_t = jnp.concatenate([jnp.where(feat < DIFF_HEAD_DIM, q_t, 0.0), jnp.where(feat >= DIFF_HEAD_DIM, q_t, 0.0)],
                           axis=1).astype(BF16)
    key_bias = slope * lax.broadcasted_iota(jnp.int32, (t, 1), 0).astype(F32)

    def step(j, carry, bias):
        m, l, acc = carry
        start = pl.multiple_of(j * t, t)
        s = _dot(k_ref[pl.ds(start, t), :], qq_t) + bias
        m_new = jnp.maximum(m, jnp.max(s, axis=0, keepdims=True))
        alpha = jnp.exp2(m - m_new)
        p = jnp.exp2(s - m_new)
        l = alpha * l + jnp.sum(p, axis=0, keepdims=True)
        acc = alpha * acc + _dot(vt_ref[j], p.astype(BF16))
        return m_new, l, acc

    def off_diagonal(j, carry):
        return step(j, carry, key_bias + slope * ((j - i) * t).astype(F32))

    carry = (jnp.full((1, 2 * t), -jnp.inf, F32), jnp.zeros((1, 2 * t), F32), jnp.zeros((DIFF_V_DIM, 2 * t), F32))
    carry = lax.fori_loop(0, i, off_diagonal, carry)
    dbias = dbias_ref[...]
    _, l, acc = step(i, carry, jnp.concatenate([dbias, dbias], axis=1))

    o = acc / l
    lam = (jnp.exp(jnp.sum(lq1_ref[...] * lk1_ref[...], axis=-1, keepdims=True))
           - jnp.exp(jnp.sum(lq2_ref[...] * lk2_ref[...], axis=-1, keepdims=True)) + lambda_init)
    o = o[:, :t] - lam * o[:, t:]
    o = o * lax.rsqrt(jnp.mean(o * o, axis=0, keepdims=True) + LN_EPS)
    o_ref[...] = (o.T * sw_ref[...] * (1.0 - lambda_init)).astype(o_ref.dtype)


def _diff_attention(qkv, lq1, lk1, lq2, lk2, subln_w, lambda_init, bsz, seq):
    slopes = LOG2_E * jnp.exp2(-8.0 * jnp.arange(1, DIFF_HEADS + 1, dtype=F32) / DIFF_HEADS)
    vec = lambda p: p.astype(F32).reshape(1, -1)
    head_blk = 2 * DIFF_HEAD_DIM
    small = lambda n: pl.BlockSpec((1, n), lambda h, b, i: (0, 0))
    return pl.pallas_call(
        functools.partial(_attn_kernel, lambda_init=lambda_init),
        grid=(DIFF_HEADS, bsz, seq // ATT_T),
        in_specs=[pl.BlockSpec(memory_space=pltpu.SMEM),
                  pl.BlockSpec((None, ATT_T, head_blk), lambda h, b, i: (b, i, h)),
                  pl.BlockSpec((None, seq, head_blk), lambda h, b, i: (b, 0, DIFF_HEADS + h)),
                  pl.BlockSpec((None, seq, DIFF_V_DIM), lambda h, b, i: (b, 0, 2 * DIFF_HEADS + h)),
                  small(DIFF_HEAD_DIM), small(DIFF_HEAD_DIM), small(DIFF_HEAD_DIM), small(DIFF_HEAD_DIM),
                  small(DIFF_V_DIM)],
        out_specs=pl.BlockSpec((None, ATT_T, DIFF_V_DIM), lambda h, b, i: (b, i, h)),
        out_shape=jax.ShapeDtypeStruct((bsz, seq, DIFF_HEADS * DIFF_V_DIM), BF16),
        scratch_shapes=[pltpu.VMEM((seq // ATT_T, DIFF_V_DIM, ATT_T), BF16),
                        pltpu.VMEM((ATT_T, ATT_T), F32)],
        compiler_params=_params("arbitrary", "arbitrary", "arbitrary"), name="diff_attention",
    )(slopes, qkv, qkv, qkv, vec(lq1), vec(lk1), vec(lq2), vec(lk2), vec(subln_w))


def kernel(x, even_w_in, even_conv_w, even_conv_b, even_dt_bias, even_a_log, even_d, even_ssd_norm_w, even_sgu_ln_g, even_sgu_ln_b, even_sgu_w, even_sgu_b, even_w_out, odd_w_qkv, odd_lambda_q1, odd_lambda_k1, odd_lambda_q2, odd_lambda_k2, odd_subln_w, odd_w_out, ln_mix_g, ln_mix_b, ln_ffn_g, ln_ffn_b, mlp_w_up, mlp_w_down):
    bsz, seq, d = x.shape
    h = x.reshape(bsz * seq, d)
    row = lambda p: p.astype(F32).reshape(1, -1)
    for l in range(DEPTH):
        i = l // 2
        if l % 2 == 0:
            w_in = even_w_in[i]
            o_z, o_xbc, o_dt, o_sgu = 0, SSD_INNER, SSD_INNER + SSD_CONV_DIM, SSD_INNER + SSD_CONV_DIM + SSD_HEADS
            w_main = jnp.concatenate([w_in[:, o_z:o_xbc], w_in[:, o_sgu:], w_in[:, o_xbc:o_dt]], axis=1).astype(BF16)
            w_dt = jnp.pad(w_in[:, o_dt:o_sgu], ((0, 0), (0, DT_PAD - SSD_HEADS))).astype(BF16)
            proj, dt_raw = _matmul(h, w_main, PROJ_TM, PROJ_TN, BF16, "in_proj", w_dt=w_dt)
            y_a = _ssd_mixer(proj, dt_raw, even_conv_w[i], even_conv_b[i], even_dt_bias[i], even_a_log[i],
                             even_d[i], even_ssd_norm_w[i], bsz, seq)
            y_b = _sgu_mixer(proj, even_sgu_ln_g[i], even_sgu_ln_b[i], even_sgu_w[i], even_sgu_b[i])
            w_out = even_w_out[i].astype(BF16)
            h = _proj_ln([y_a, y_b], [w_out[:SSD_INNER], w_out[SSD_INNER:]], h, row(ln_mix_g[l]), row(ln_mix_b[l]),
                         "even_out_proj_ln")
        else:
            lambda_init = 0.8 - 0.6 * math.exp(-0.3 * l)
            w_qkv = odd_w_qkv[i]
            q_scale = DIFF_HEAD_DIM ** -0.5 * LOG2_E
            w_qkv = jnp.concatenate([w_qkv[:, :DIFF_QK] * q_scale, w_qkv[:, DIFF_QK:]], axis=1).astype(BF16)
            qkv = _matmul(h, w_qkv, PROJ_TM, PROJ_TN, BF16, "qkv_proj").reshape(bsz, seq, -1)
            o = _diff_attention(qkv, odd_lambda_q1[i], odd_lambda_k1[i], odd_lambda_q2[i], odd_lambda_k2[i],
                                odd_subln_w[i], lambda_init, bsz, seq)
            h = _proj_ln([o.reshape(bsz * seq, -1)], [odd_w_out[i].astype(BF16)], h, row(ln_mix_g[l]), row(ln_mix_b[l]),
                         "odd_out_proj_ln")
        h = _mlp_ln(h, mlp_w_up[l].astype(BF16), mlp_w_down[l].astype(BF16), row(ln_ffn_g[l]), row(ln_ffn_b[l]),
                    "mlp_ln_%d" % l)
    return h.reshape(bsz, seq, d)
```

```python
import functools
import math

import jax
import jax.numpy as jnp
from jax import lax
from jax.experimental import pallas as pl
from jax.experimental.pallas import tpu as pltpu

F32 = jnp.float32
BF16 = jnp.bfloat16

D_MODEL = 1024
DEPTH = 2
CHUNK = 64
SSD_HEADS = 16
SSD_HEAD_DIM = 64
SSD_INNER = SSD_HEADS * SSD_HEAD_DIM
SSD_GROUPS = 2
SSD_STATE = 128
SSD_CONV = 4
SSD_BC = SSD_GROUPS * SSD_STATE
SSD_CONV_DIM = SSD_INNER + 2 * SSD_BC
SSD_GROUP_WIDTH = SSD_INNER // SSD_GROUPS
SGU_BLOCK = 128
SGU_GROUPS = 8
SGU_WIDTH = 1024
SGU_GROUP_DIM = SGU_WIDTH // SGU_GROUPS
DIFF_HEADS = 8
DIFF_HEAD_DIM = 64
DIFF_V_DIM = 2 * DIFF_HEAD_DIM
DIFF_QK = DIFF_HEADS * 2 * DIFF_HEAD_DIM
D_FF = 4 * D_MODEL
DEEPNORM_ALPHA = (2 * DEPTH) ** 0.25
LN_EPS = 1e-5

V7X_LANES = 128
V7X_SUBLANES = 8
V7X_VMEM_BYTES = 64 * 1024 * 1024
VMEM_LIMIT = 56 * 1024 * 1024

PROJ_TM = 1024
PROJ_TN = 1536
LN_TM = 512
MLP_TM = 512
MLP_FF_CHUNK = 1024
SSD_L = 256
SGU_TM = 512
ATT_T = 512
ATT_ONES_ROWS = 16
ATT_FEAT = V7X_LANES
LOG2_E = 1.0 / math.log(2.0)
DT_PAD = V7X_LANES


def _params(*sem):
    return pltpu.CompilerParams(dimension_semantics=sem, vmem_limit_bytes=VMEM_LIMIT)


def _layer_norm(y, g, b):
    mu = jnp.mean(y, axis=-1, keepdims=True)
    yc = y - mu
    var = jnp.mean(yc * yc, axis=-1, keepdims=True)
    return yc * lax.rsqrt(var + LN_EPS) * g + b


def _dot(a, b):
    return jnp.dot(a, b, preferred_element_type=F32)


def _dot_nt(a, b):
    return lax.dot_general(a, b, (((1,), (1,)), ((), ())), preferred_element_type=F32)


def _matmul_kernel(x_ref, w_ref, o_ref):
    o_ref[...] = _dot(x_ref[...].astype(BF16), w_ref[...]).astype(o_ref.dtype)


def _matmul_dt_kernel(x_ref, w_ref, wdt_ref, o_ref, dt_ref):
    xb = x_ref[...].astype(BF16)
    o_ref[...] = _dot(xb, w_ref[...]).astype(o_ref.dtype)

    @pl.when(pl.program_id(1) == 0)
    def _():
        dt_ref[...] = _dot(xb, wdt_ref[...])


def _matmul(x, w, tm, tn, out_dtype, name, w_dt=None):
    t, k = x.shape
    n = w.shape[1]
    grid = (t // tm, n // tn)
    x_spec = pl.BlockSpec((tm, k), lambda i, j: (i, 0))
    w_spec = pl.BlockSpec((k, tn), lambda i, j: (0, j))
    o_spec = pl.BlockSpec((tm, tn), lambda i, j: (i, j))
    if w_dt is None:
        return pl.pallas_call(
            _matmul_kernel, grid=grid, in_specs=[x_spec, w_spec], out_specs=o_spec,
            out_shape=jax.ShapeDtypeStruct((t, n), out_dtype),
            compiler_params=_params("parallel", "arbitrary"), name=name)(x, w)
    return pl.pallas_call(
        _matmul_dt_kernel, grid=grid,
        in_specs=[x_spec, w_spec, pl.BlockSpec((k, DT_PAD), lambda i, j: (0, 0))],
        out_specs=[o_spec, pl.BlockSpec((tm, DT_PAD), lambda i, j: (i, 0))],
        out_shape=[jax.ShapeDtypeStruct((t, n), out_dtype), jax.ShapeDtypeStruct((t, DT_PAD), F32)],
        compiler_params=_params("parallel", "arbitrary"), name=name)(x, w, w_dt)


def _proj_ln_kernel(*refs, n_in):
    x_refs, w_refs = refs[:n_in], refs[n_in:2 * n_in]
    h_ref, g_ref, b_ref, o_ref = refs[2 * n_in:]
    acc = _dot(x_refs[0][...], w_refs[0][...])
    for x_ref, w_ref in zip(x_refs[1:], w_refs[1:]):
        acc += _dot(x_ref[...], w_ref[...])
    o_ref[...] = _layer_norm(DEEPNORM_ALPHA * h_ref[...] + acc, g_ref[...], b_ref[...])


def _proj_ln(xs, ws, h, g, b, name):
    t, d = h.shape
    tm = LN_TM
    n_in = len(xs)
    in_specs = ([pl.BlockSpec((tm, x.shape[1]), lambda i: (i, 0)) for x in xs]
                + [pl.BlockSpec(w.shape, lambda i: (0, 0)) for w in ws]
                + [pl.BlockSpec((tm, d), lambda i: (i, 0)),
                   pl.BlockSpec((1, d), lambda i: (0, 0)), pl.BlockSpec((1, d), lambda i: (0, 0))])
    return pl.pallas_call(
        functools.partial(_proj_ln_kernel, n_in=n_in), grid=(t // tm,), in_specs=in_specs,
        out_specs=pl.BlockSpec((tm, d), lambda i: (i, 0)),
        out_shape=jax.ShapeDtypeStruct((t, d), F32),
        compiler_params=_params("parallel"), name=name)(*xs, *ws, h, g, b)


def _mlp_ln_kernel(h_ref, wu_ref, wd_ref, g_ref, b_ref, o_ref):
    h = h_ref[...]
    hb = h.astype(BF16)
    acc = jnp.zeros(h.shape, F32)
    for c in range(D_FF // MLP_FF_CHUNK):
        cols = slice(c * MLP_FF_CHUNK, (c + 1) * MLP_FF_CHUNK)
        u = jnp.maximum(_dot(hb, wu_ref[:, cols]), 0.0)
        acc += _dot((u * u).astype(BF16), wd_ref[cols, :])
    o_ref[...] = _layer_norm(DEEPNORM_ALPHA * h + acc, g_ref[...], b_ref[...])


def _mlp_ln(h, w_up, w_down, g, b, name):
    t, d = h.shape
    tm = MLP_TM
    return pl.pallas_call(
        _mlp_ln_kernel, grid=(t // tm,),
        in_specs=[pl.BlockSpec((tm, d), lambda i: (i, 0)),
                  pl.BlockSpec(w_up.shape, lambda i: (0, 0)),
                  pl.BlockSpec(w_down.shape, lambda i: (0, 0)),
                  pl.BlockSpec((1, d), lambda i: (0, 0)), pl.BlockSpec((1, d), lambda i: (0, 0))],
        out_specs=pl.BlockSpec((tm, d), lambda i: (i, 0)),
        out_shape=jax.ShapeDtypeStruct((t, d), F32),
        compiler_params=_params("parallel"), name=name)(h, w_up, w_down, g, b)


def _softplus(x):
    return jnp.maximum(x, 0.0) + jnp.log1p(jnp.exp(-jnp.abs(x)))


def _silu(x):
    return x * jax.nn.sigmoid(x)


def _cumsum_rows(x):
    n = x.shape[0]
    row = lax.broadcasted_iota(jnp.int32, x.shape, 0)
    shift = 1
    while shift < n:
        x = x + jnp.where(row >= shift, pltpu.roll(x, shift, 0), 0.0)
        shift *= 2
    return x


def _ssd_kernel(z_ref, xbc_ref, dtr_ref, cw_ref, cb_ref, dtb_ref, alog_ref, dskip_ref, nw_ref, e_ref,
                o_ref, ext_ref, state_ref):
    L = SSD_L
    pad = V7X_SUBLANES

    @pl.when(pl.program_id(1) == 0)
    def _():
        ext_ref[0:pad, :] = jnp.zeros((pad, SSD_CONV_DIM), F32)
        state_ref[...] = jnp.zeros(state_ref.shape, F32)

    ext_ref[pad:, :] = xbc_ref[...].astype(F32)
    conv = cb_ref[...] + cw_ref[0:1, :] * ext_ref[pad - 3:pad - 3 + L, :]
    for k in range(1, SSD_CONV):
        conv += cw_ref[k:k + 1, :] * ext_ref[pad - 3 + k:pad - 3 + k + L, :]
    ext_ref[0:pad, :] = ext_ref[L:L + pad, :]
    xbc = _silu(conv)
    xs = xbc[:, :SSD_INNER]
    bmat = xbc[:, SSD_INNER:SSD_INNER + SSD_BC]
    cmat = xbc[:, SSD_INNER + SSD_BC:]

    dt = _softplus(dtr_ref[...] + dtb_ref[...])
    a_cs = _cumsum_rows(dt * (-jnp.exp(alog_ref[...])))
    a_last = a_cs[L - 1:L, :]
    exp_a = jnp.exp(a_cs)
    to_end = jnp.exp(a_last - a_cs)
    stacked = jnp.concatenate([dt, exp_a, dt * to_end], axis=0)
    expanded = jnp.dot(stacked, e_ref[...], preferred_element_type=F32, precision=lax.Precision.HIGHEST)
    dt_e, exp_a_e, w_end_e = expanded[:L], expanded[L:2 * L], expanded[2 * L:]

    xdt_b = (xs * dt_e).astype(BF16)
    xend_b = (xs * w_end_e).astype(BF16)
    b_b = bmat.astype(BF16)
    c_b = cmat.astype(BF16)
    a_cs_t = a_cs.T
    causal = (lax.broadcasted_iota(jnp.int32, (L, L), 0) >= lax.broadcasted_iota(jnp.int32, (L, L), 1))
    lane = lax.broadcasted_iota(jnp.int32, (L, V7X_LANES), 1)
    heads_per_group = SSD_HEADS // SSD_GROUPS

    y_parts = []
    for g in range(SSD_GROUPS):
        gs = slice(g * SSD_STATE, (g + 1) * SSD_STATE)
        gw = slice(g * SSD_GROUP_WIDTH, (g + 1) * SSD_GROUP_WIDTH)
        cb = _dot_nt(c_b[:, gs], b_b[:, gs])
        y_off = _dot(c_b[:, gs], state_ref[g].astype(BF16)) * exp_a_e[:, gw]
        for pair in range(heads_per_group // 2):
            h0 = g * heads_per_group + 2 * pair
            cols = slice(h0 * SSD_HEAD_DIM, (h0 + 2) * SSD_HEAD_DIM)
            x_pair = xdt_b[:, cols]
            y_pair = None
            for sub in range(2):
                h = h0 + sub
                seg = a_cs[:, h:h + 1] - a_cs_t[h:h + 1, :]
                decay = jnp.exp(jnp.where(causal, seg, -jnp.inf))
                m = (cb * decay).astype(BF16)
                keep = (lane < SSD_HEAD_DIM) if sub == 0 else (lane >= SSD_HEAD_DIM)
                part = _dot(m, jnp.where(keep, x_pair, jnp.zeros_like(x_pair)))
                y_pair = part if y_pair is None else y_pair + part
            y_parts.append(y_pair + y_off[:, 2 * pair * SSD_HEAD_DIM:(2 * pair + 2) * SSD_HEAD_DIM])
        upd = lax.dot_general(b_b[:, gs], xend_b[:, gw], (((0,), (0,)), ((), ())), preferred_element_type=F32)
        state_ref[g] = state_ref[g] * exp_a_e[L - 1:L, gw] + upd

    y = jnp.concatenate(y_parts, axis=1) + xs * dskip_ref[...]
    y = y * _silu(z_ref[...].astype(F32))
    normed = []
    for g in range(SSD_GROUPS):
        yg = y[:, g * SSD_GROUP_WIDTH:(g + 1) * SSD_GROUP_WIDTH]
        normed.append(yg * lax.rsqrt(jnp.mean(yg * yg, axis=-1, keepdims=True) + LN_EPS))
    o_ref[...] = (jnp.concatenate(normed, axis=1) * nw_ref[...]).astype(o_ref.dtype)


def _ssd_mixer(proj, dt_raw, conv_w, conv_b, dt_bias, a_log, d_skip, norm_w, bsz, seq):
    steps = seq // SSD_L
    row = lambda b, s: b * steps + s
    pad16 = lambda v: jnp.pad(v.astype(F32), (0, DT_PAD - SSD_HEADS)).reshape(1, DT_PAD)
    head_of_channel = jnp.arange(SSD_INNER) // SSD_HEAD_DIM
    expand = (jnp.arange(DT_PAD)[:, None] == head_of_channel[None, :]).astype(F32)
    d_e = jnp.repeat(d_skip.astype(F32), SSD_HEAD_DIM).reshape(1, SSD_INNER)
    const = lambda shape: pl.BlockSpec(shape, lambda b, s: (0, 0))
    return pl.pallas_call(
        _ssd_kernel, grid=(bsz, steps),
        in_specs=[pl.BlockSpec((SSD_L, SSD_INNER), lambda b, s: (row(b, s), 0)),
                  pl.BlockSpec((SSD_L, SSD_CONV_DIM), lambda b, s: (row(b, s), 2)),
                  pl.BlockSpec((SSD_L, DT_PAD), lambda b, s: (row(b, s), 0)),
                  const((SSD_CONV, SSD_CONV_DIM)), const((1, SSD_CONV_DIM)),
                  const((1, DT_PAD)), const((1, DT_PAD)),
                  const((1, SSD_INNER)), const((1, SSD_INNER)), const((DT_PAD, SSD_INNER))],
        out_specs=pl.BlockSpec((SSD_L, SSD_INNER), lambda b, s: (row(b, s), 0)),
        out_shape=jax.ShapeDtypeStruct((bsz * seq, SSD_INNER), BF16),
        scratch_shapes=[pltpu.VMEM((SSD_L + V7X_SUBLANES, SSD_CONV_DIM), F32),
                        pltpu.VMEM((SSD_GROUPS, SSD_STATE, SSD_GROUP_WIDTH), F32)],
        compiler_params=_params("parallel", "arbitrary"), name="ssd_mixer",
    )(proj, proj, dt_raw, conv_w.astype(F32), conv_b.astype(F32).reshape(1, -1), pad16(dt_bias), pad16(a_log),
      d_e, norm_w.astype(F32).reshape(1, -1), expand)


def _gelu(x):
    return 0.5 * x * (1.0 + lax.erf(x * (1.0 / math.sqrt(2.0))))


def _sgu_kernel(u_ref, v_ref, g_ref, b_ref, w_ref, bs_ref, o_ref):
    v = _layer_norm(_gelu(v_ref[...].astype(F32)), g_ref[...], b_ref[...]).astype(BF16)
    r = lax.broadcasted_iota(jnp.int32, (SGU_BLOCK, SGU_BLOCK), 0) // CHUNK
    c = lax.broadcasted_iota(jnp.int32, (SGU_BLOCK, SGU_BLOCK), 1) // CHUNK
    for g in range(SGU_GROUPS):
        w = jnp.where(r >= c, w_ref[g], 0.0).astype(BF16)
        cols = slice(g * SGU_GROUP_DIM, (g + 1) * SGU_GROUP_DIM)
        for blk in range(SGU_TM // SGU_BLOCK):
            rows = slice(blk * SGU_BLOCK, (blk + 1) * SGU_BLOCK)
            mixed = _dot(w, v[rows, cols]) + bs_ref[g]
            o_ref[rows, cols] = (_gelu(u_ref[rows, cols].astype(F32)) * mixed).astype(o_ref.dtype)


def _sgu_mixer(proj, ln_g, ln_b, w_s, b_s):
    t = proj.shape[0]
    bias = jnp.broadcast_to(b_s.astype(F32)[:, :, None], (SGU_GROUPS, SGU_BLOCK, SGU_GROUP_DIM))
    return pl.pallas_call(
        _sgu_kernel, grid=(t // SGU_TM,),
        in_specs=[pl.BlockSpec((SGU_TM, SGU_WIDTH), lambda i: (i, 1)),
                  pl.BlockSpec((SGU_TM, SGU_WIDTH), lambda i: (i, 2)),
                  pl.BlockSpec((1, SGU_WIDTH), lambda i: (0, 0)), pl.BlockSpec((1, SGU_WIDTH), lambda i: (0, 0)),
                  pl.BlockSpec((SGU_GROUPS, SGU_BLOCK, SGU_BLOCK), lambda i: (0, 0, 0)),
                  pl.BlockSpec((SGU_GROUPS, SGU_BLOCK, SGU_GROUP_DIM), lambda i: (0, 0, 0))],
        out_specs=pl.BlockSpec((SGU_TM, SGU_WIDTH), lambda i: (i, 0)),
        out_shape=jax.ShapeDtypeStruct((t, SGU_WIDTH), BF16),
        compiler_params=_params("parallel"), name="sgu_mixer",
    )(proj, proj, ln_g.astype(F32).reshape(1, -1), ln_b.astype(F32).reshape(1, -1), w_s.astype(F32), bias)


def _attn_kernel(slopes_ref, q_ref, k_ref, v_ref, lq1_ref, lk1_ref, lq2_ref, lk2_ref, sw_ref, o_ref,
                 vt_ref, dbias_ref, kfeat_ref, qq_ref, s0_ref, s1_ref, cmax0_ref, cmax1_ref, m_ref, acc_ref,
                 *, lambda_init):
    t = ATT_T
    s_refs, cmax_refs = (s0_ref, s1_ref), (cmax0_ref, cmax1_ref)
    head, b, i = pl.program_id(0), pl.program_id(1), pl.program_id(2)
    slope = slopes_ref[head]

    @pl.when(i == 0)
    def _():
        ones = jnp.ones((ATT_ONES_ROWS, t), BF16)
        for c in range(vt_ref.shape[0]):
            vt_ref[c, :DIFF_V_DIM, :] = v_ref[c * t:(c + 1) * t, :].astype(F32).T.astype(BF16)
            vt_ref[c, DIFF_V_DIM:, :] = ones
        feat = lax.broadcasted_iota(jnp.int32, (ATT_FEAT, 2 * t), 0)
        qq_ref[2 * DIFF_HEAD_DIM:, :] = jnp.where(feat < 3, 1.0, 0.0).astype(F32).astype(BF16)

    @pl.when((i == 0) & (b == 0))
    def _():
        key = lax.broadcasted_iota(jnp.int32, (t, t), 0)
        qry = lax.broadcasted_iota(jnp.int32, (t, t), 1)
        dbias_ref[...] = jnp.where(key // CHUNK <= qry // CHUNK,
                                   (-2.0 * slope) * jnp.maximum(key - qry, 0).astype(F32), -jnp.inf)
        bias = slope * lax.broadcasted_iota(jnp.int32, (t, ATT_FEAT), 0).astype(F32)
        hi = bias.astype(BF16).astype(F32)
        mid = (bias - hi).astype(BF16).astype(F32)
        lo = bias - hi - mid
        col = lax.broadcasted_iota(jnp.int32, (t, ATT_FEAT), 1)
        kfeat_ref[...] = jnp.where(col == 0, hi, jnp.where(col == 1, mid, jnp.where(col == 2, lo, 0.0))).astype(BF16)

    q_t = q_ref[...].astype(F32).T
    feat = lax.broadcasted_iota(jnp.int32, q_t.shape, 0)
    qq_ref[:2 * DIFF_HEAD_DIM, :t] = jnp.where(feat < DIFF_HEAD_DIM, q_t, 0.0).astype(BF16)
    qq_ref[:2 * DIFF_HEAD_DIM, t:] = jnp.where(feat >= DIFF_HEAD_DIM, q_t, 0.0).astype(BF16)
    m_ref[...] = jnp.full(m_ref.shape, -jnp.inf, F32)
    acc_ref[...] = jnp.zeros(acc_ref.shape, F32)

    def scores(j, slot, diagonal):
        start = pl.multiple_of(j * t, t)
        lhs = jnp.concatenate([k_ref[pl.ds(start, t), :], kfeat_ref[...]], axis=1)
        s = _dot(lhs, qq_ref[...])
        if diagonal:
            dbias = dbias_ref[...]
            s = s + jnp.concatenate([dbias, dbias], axis=1)
        s_refs[slot][...] = s
        cmax_refs[slot][...] = jnp.max(s, axis=0, keepdims=True)

    def accumulate(j, slot):
        shift = slope * ((j - i) * t).astype(F32)
        m = m_ref[...]
        m_new = jnp.maximum(m, cmax_refs[slot][...] + shift)
        alpha = jnp.exp2(m - m_new)
        p = jnp.exp2(s_refs[slot][...] - (m_new - shift))
        m_ref[...] = m_new
        acc_ref[...] = alpha * acc_ref[...] + _dot(vt_ref[j], p.astype(BF16))

    key_tile = lambda n: jnp.where(n == 0, i, n - 1)
    scores(i, 0, diagonal=True)

    def body(pair, carry):
        n = 2 * pair
        scores(key_tile(n + 1), 1, diagonal=False)
        accumulate(key_tile(n), 0)
        scores(key_tile(n + 2), 0, diagonal=False)
        accumulate(key_tile(n + 1), 1)
        return carry

    pairs = i // 2
    lax.fori_loop(0, pairs, body, 0)
    odd = i % 2 == 1

    @pl.when(odd)
    def _():
        scores(i - 1, 1, diagonal=False)

    accumulate(key_tile(2 * pairs), 0)

    @pl.when(odd)
    def _():
        accumulate(i - 1, 1)

    o = acc_ref[:DIFF_V_DIM, :] / acc_ref[DIFF_V_DIM:DIFF_V_DIM + 1, :]
    lam =(jnp.exp(jnp.sum(lq1_ref[...] * lk1_ref[...], axis=-1, keepdims=True))
           - jnp.exp(jnp.sum(lq2_ref[...] * lk2_ref[...], axis=-1, keepdims=True)) + lambda_init)
    o = o[:, :t] - lam * o[:, t:]
    o = o * lax.rsqrt(jnp.mean(o * o, axis=0, keepdims=True) + LN_EPS)
    o_ref[...] = (o.T * sw_ref[...] * (1.0 - lambda_init)).astype(o_ref.dtype)


def _diff_attention(qkv, lq1, lk1, lq2, lk2, subln_w, lambda_init, bsz, seq):
    slopes = LOG2_E * jnp.exp2(-8.0 * jnp.arange(1, DIFF_HEADS + 1, dtype=F32) / DIFF_HEADS)
    vec = lambda p: p.astype(F32).reshape(1, -1)
    head_blk = 2 * DIFF_HEAD_DIM
    small = lambda n: pl.BlockSpec((1, n), lambda h, b, i: (0, 0))
    return pl.pallas_call(
        functools.partial(_attn_kernel, lambda_init=lambda_init),
        grid=(DIFF_HEADS, bsz, seq // ATT_T),
        in_specs=[pl.BlockSpec(memory_space=pltpu.SMEM),
                  pl.BlockSpec((None, ATT_T, head_blk), lambda h, b, i: (b, i, h)),
                  pl.BlockSpec((None, seq, head_blk), lambda h, b, i: (b, 0, DIFF_HEADS + h)),
                  pl.BlockSpec((None, seq, DIFF_V_DIM), lambda h, b, i: (b, 0, 2 * DIFF_HEADS + h)),
                  small(DIFF_HEAD_DIM), small(DIFF_HEAD_DIM), small(DIFF_HEAD_DIM), small(DIFF_HEAD_DIM),
                  small(DIFF_V_DIM)],
        out_specs=pl.BlockSpec((None, ATT_T, DIFF_V_DIM), lambda h, b, i: (b, i, h)),
        out_shape=jax.ShapeDtypeStruct((bsz, seq, DIFF_HEADS * DIFF_V_DIM), BF16),
        scratch_shapes=[pltpu.VMEM((seq // ATT_T, DIFF_V_DIM + ATT_ONES_ROWS, ATT_T), BF16),
                        pltpu.VMEM((ATT_T, ATT_T), F32),
                        pltpu.VMEM((ATT_T, ATT_FEAT), BF16),
                        pltpu.VMEM((DIFF_V_DIM + ATT_FEAT, 2 * ATT_T), BF16),
                        pltpu.VMEM((ATT_T, 2 * ATT_T), F32), pltpu.VMEM((ATT_T, 2 * ATT_T), F32),
                        pltpu.VMEM((1, 2 * ATT_T), F32), pltpu.VMEM((1, 2 * ATT_T), F32),
                        pltpu.VMEM((1, 2 * ATT_T), F32),
                        pltpu.VMEM((DIFF_V_DIM + ATT_ONES_ROWS, 2 * ATT_T), F32)],
        compiler_params=_params("arbitrary", "arbitrary", "arbitrary"), name="diff_attention",
    )(slopes, qkv, qkv, qkv, vec(lq1), vec(lk1), vec(lq2), vec(lk2), vec(subln_w))


def kernel(x, even_w_in, even_conv_w, even_conv_b, even_dt_bias, even_a_log, even_d, even_ssd_norm_w, even_sgu_ln_g, even_sgu_ln_b, even_sgu_w, even_sgu_b, even_w_out, odd_w_qkv, odd_lambda_q1, odd_lambda_k1, odd_lambda_q2, odd_lambda_k2, odd_subln_w, odd_w_out, ln_mix_g, ln_mix_b, ln_ffn_g, ln_ffn_b, mlp_w_up, mlp_w_down):
    bsz, seq, d = x.shape
    h = x.reshape(bsz * seq, d)
    row = lambda p: p.astype(F32).reshape(1, -1)
    for l in range(DEPTH):
        i = l // 2
        if l % 2 == 0:
            w_in = even_w_in[i]
            o_z, o_xbc, o_dt, o_sgu = 0, SSD_INNER, SSD_INNER + SSD_CONV_DIM, SSD_INNER + SSD_CONV_DIM + SSD_HEADS
            w_main = jnp.concatenate([w_in[:, o_z:o_xbc], w_in[:, o_sgu:], w_in[:, o_xbc:o_dt]], axis=1).astype(BF16)
            w_dt = jnp.pad(w_in[:, o_dt:o_sgu], ((0, 0), (0, DT_PAD - SSD_HEADS))).astype(BF16)
            proj, dt_raw = _matmul(h, w_main, PROJ_TM, PROJ_TN, BF16, "in_proj", w_dt=w_dt)
            y_a = _ssd_mixer(proj, dt_raw, even_conv_w[i], even_conv_b[i], even_dt_bias[i], even_a_log[i],
                             even_d[i], even_ssd_norm_w[i], bsz, seq)
            y_b = _sgu_mixer(proj, even_sgu_ln_g[i], even_sgu_ln_b[i], even_sgu_w[i], even_sgu_b[i])
            w_out = even_w_out[i].astype(BF16)
            h = _proj_ln([y_a, y_b], [w_out[:SSD_INNER], w_out[SSD_INNER:]], h, row(ln_mix_g[l]), row(ln_mix_b[l]),
                         "even_out_proj_ln")
        else:
            lambda_init = 0.8 - 0.6 * math.exp(-0.3 * l)
            w_qkv = odd_w_qkv[i]
            q_scale = DIFF_HEAD_DIM ** -0.5 * LOG2_E
            w_qkv = jnp.concatenate([w_qkv[:, :DIFF_QK] * q_scale, w_qkv[:, DIFF_QK:]], axis=1).astype(BF16)
            qkv = _matmul(h, w_qkv, PROJ_TM, PROJ_TN, BF16, "qkv_proj").reshape(bsz, seq, -1)
            o = _diff_attention(qkv, odd_lambda_q1[i], odd_lambda_k1[i], odd_lambda_q2[i], odd_lambda_k2[i],
                                odd_subln_w[i], lambda_init, bsz, seq)
            h = _proj_ln([o.reshape(bsz * seq, -1)], [odd_w_out[i].astype(BF16)], h, row(ln_mix_g[l]), row(ln_mix_b[l]),
                         "odd_out_proj_ln")
        h = _mlp_ln(h, mlp_w_up[l].astype(BF16), mlp_w_down[l].astype(BF16), row(ln_ffn_g[l]), row(ln_ffn_b[l]),
                    "mlp_ln_%d" % l)
    return h.reshape(bsz, seq, d)
```

```python
import functools
import math

import jax
import jax.numpy as jnp
from jax import lax
from jax.experimental import pallas as pl
from jax.experimental.pallas import tpu as pltpu

F32 = jnp.float32
BF16 = jnp.bfloat16

D_MODEL = 1024
DEPTH = 2
CHUNK = 64
SSD_HEADS = 16
SSD_HEAD_DIM = 64
SSD_INNER = SSD_HEADS * SSD_HEAD_DIM
SSD_GROUPS = 2
SSD_STATE = 128
SSD_CONV = 4
SSD_BC = SSD_GROUPS * SSD_STATE
SSD_CONV_DIM = SSD_INNER + 2 * SSD_BC
SSD_GROUP_WIDTH = SSD_INNER // SSD_GROUPS
SGU_BLOCK = 128
SGU_GROUPS = 8
SGU_WIDTH = 1024
SGU_GROUP_DIM = SGU_WIDTH // SGU_GROUPS
DIFF_HEADS = 8
DIFF_HEAD_DIM = 64
DIFF_V_DIM = 2 * DIFF_HEAD_DIM
DIFF_QK = DIFF_HEADS * 2 * DIFF_HEAD_DIM
D_FF = 4 * D_MODEL
DEEPNORM_ALPHA = (2 * DEPTH) ** 0.25
LN_EPS = 1e-5

V7X_LANES = 128
V7X_SUBLANES = 8
V7X_VMEM_BYTES = 64 * 1024 * 1024
VMEM_LIMIT = 56 * 1024 * 1024

PROJ_TM = 1024
PROJ_TN = 1536
LN_TM = 512
MLP_TM = 512
MLP_FF_CHUNK = 1024
SSD_L = 256
SSD_CHUNK = 128
SGU_TM = 512
ATT_T = 512
ATT_ONES_ROWS = 16
ATT_FEAT = V7X_LANES
LOG2_E = 1.0 / math.log(2.0)
DT_PAD = V7X_LANES


def _params(*sem):
    return pltpu.CompilerParams(dimension_semantics=sem, vmem_limit_bytes=VMEM_LIMIT)


def _layer_norm(y, g, b):
    mu = jnp.mean(y, axis=-1, keepdims=True)
    yc = y - mu
    var = jnp.mean(yc * yc, axis=-1, keepdims=True)
    return yc * lax.rsqrt(var + LN_EPS) * g + b


def _dot(a, b):
    return jnp.dot(a, b, preferred_element_type=F32)


def _dot_nt(a, b):
    return lax.dot_general(a, b, (((1,), (1,)), ((), ())), preferred_element_type=F32)


def _matmul_kernel(x_ref, w_ref, o_ref):
    o_ref[...] = _dot(x_ref[...].astype(BF16), w_ref[...]).astype(o_ref.dtype)


def _matmul_dt_kernel(x_ref, w_ref, wdt_ref, o_ref, dt_ref):
    xb = x_ref[...].astype(BF16)
    o_ref[...] = _dot(xb, w_ref[...]).astype(o_ref.dtype)

    @pl.when(pl.program_id(1) == 0)
    def _():
        dt_ref[...] = _dot(xb, wdt_ref[...])


def _matmul(x, w, tm, tn, out_dtype, name, w_dt=None):
    t, k = x.shape
    n = w.shape[1]
    grid = (t // tm, n // tn)
    x_spec = pl.BlockSpec((tm, k), lambda i, j: (i, 0))
    w_spec = pl.BlockSpec((k, tn), lambda i, j: (0, j))
    o_spec = pl.BlockSpec((tm, tn), lambda i, j: (i, j))
    if w_dt is None:
        return pl.pallas_call(
            _matmul_kernel, grid=grid, in_specs=[x_spec, w_spec], out_specs=o_spec,
            out_shape=jax.ShapeDtypeStruct((t, n), out_dtype),
            compiler_params=_params("parallel", "arbitrary"), name=name)(x, w)
    return pl.pallas_call(
        _matmul_dt_kernel, grid=grid,
        in_specs=[x_spec, w_spec, pl.BlockSpec((k, DT_PAD), lambda i, j: (0, 0))],
        out_specs=[o_spec, pl.BlockSpec((tm, DT_PAD), lambda i, j: (i, 0))],
        out_shape=[jax.ShapeDtypeStruct((t, n), out_dtype), jax.ShapeDtypeStruct((t, DT_PAD), F32)],
        compiler_params=_params("parallel", "arbitrary"), name=name)(x, w, w_dt)


def _proj_ln_kernel(*refs, n_in):
    x_refs, w_refs = refs[:n_in], refs[n_in:2 * n_in]
    h_ref, g_ref, b_ref, o_ref = refs[2 * n_in:]
    acc = _dot(x_refs[0][...], w_refs[0][...])
    for x_ref, w_ref in zip(x_refs[1:], w_refs[1:]):
        acc += _dot(x_ref[...], w_ref[...])
    o_ref[...] = _layer_norm(DEEPNORM_ALPHA * h_ref[...] + acc, g_ref[...], b_ref[...])


def _proj_ln(xs, ws, h, g, b, name):
    t, d = h.shape
    tm = LN_TM
    n_in = len(xs)
    in_specs = ([pl.BlockSpec((tm, x.shape[1]), lambda i: (i, 0)) for x in xs]
                + [pl.BlockSpec(w.shape, lambda i: (0, 0)) for w in ws]
                + [pl.BlockSpec((tm, d), lambda i: (i, 0)),
                   pl.BlockSpec((1, d), lambda i: (0, 0)), pl.BlockSpec((1, d), lambda i: (0, 0))])
    return pl.pallas_call(
        functools.partial(_proj_ln_kernel, n_in=n_in), grid=(t // tm,), in_specs=in_specs,
        out_specs=pl.BlockSpec((tm, d), lambda i: (i, 0)),
        out_shape=jax.ShapeDtypeStruct((t, d), F32),
        compiler_params=_params("parallel"), name=name)(*xs, *ws, h, g, b)


def _mlp_ln_kernel(h_ref, wu_ref, wd_ref, g_ref, b_ref, o_ref):
    h = h_ref[...]
    hb = h.astype(BF16)
    acc = jnp.zeros(h.shape, F32)
    for c in range(D_FF // MLP_FF_CHUNK):
        cols = slice(c * MLP_FF_CHUNK, (c + 1) * MLP_FF_CHUNK)
        u = jnp.maximum(_dot(hb, wu_ref[:, cols]), 0.0)
        acc += _dot((u * u).astype(BF16), wd_ref[cols, :])
    o_ref[...] = _layer_norm(DEEPNORM_ALPHA * h + acc, g_ref[...], b_ref[...])


def _mlp_ln(h, w_up, w_down, g, b, name):
    t, d = h.shape
    tm = MLP_TM
    return pl.pallas_call(
        _mlp_ln_kernel, grid=(t // tm,),
        in_specs=[pl.BlockSpec((tm, d), lambda i: (i, 0)),
                  pl.BlockSpec(w_up.shape, lambda i: (0, 0)),
                  pl.BlockSpec(w_down.shape, lambda i: (0, 0)),
                  pl.BlockSpec((1, d), lambda i: (0, 0)), pl.BlockSpec((1, d), lambda i: (0, 0))],
        out_specs=pl.BlockSpec((tm, d), lambda i: (i, 0)),
        out_shape=jax.ShapeDtypeStruct((t, d), F32),
        compiler_params=_params("parallel"), name=name)(h, w_up, w_down, g, b)


def _softplus(x):
    return jnp.maximum(x, 0.0) + jnp.log1p(jnp.exp(-jnp.abs(x)))


def _silu(x):
    return x * jax.nn.sigmoid(x)


def _cumsum_rows(x, segment):
    row = lax.broadcasted_iota(jnp.int32, x.shape, 0) % segment
    shift = 1
    while shift < segment:
        x = x + jnp.where(row >= shift, pltpu.roll(x, shift, 0), 0.0)
        shift *= 2
    return x


def _ssd_kernel(z_ref, xbc_ref, dtr_ref, shift_ref, cw_ref, cb_ref, dtb_ref, alog_ref, dskip_ref, nw_ref, e_ref,
                o_ref, tail_ref, state_ref):
    L, C = SSD_L, SSD_CHUNK
    sub = V7X_SUBLANES

    @pl.when(pl.program_id(1) == 0)
    def _():
        tail_ref[...] = jnp.zeros(tail_ref.shape, F32)
        state_ref[...] = jnp.zeros(state_ref.shape, F32)

    x_b = xbc_ref[...]
    x_f = x_b.astype(F32)
    conv = cb_ref[...] + cw_ref[SSD_CONV - 1:SSD_CONV, :] * x_f
    tail = tail_ref[...]
    row8 = lax.broadcasted_iota(jnp.int32, tail.shape, 0)
    head_fix = jnp.zeros(tail.shape, F32)
    for d in range(1, SSD_CONV):
        w = cw_ref[SSD_CONV - 1 - d:SSD_CONV - d, :]
        conv += w * _dot(shift_ref[d - 1], x_b)
        head_fix += w * jnp.where(row8 < d, pltpu.roll(tail, d, 0), 0.0)
    conv = jnp.concatenate([conv[:sub] + head_fix, conv[sub:]], axis=0)
    tail_ref[...] = x_f[L - sub:, :]
    xbc = _silu(conv)
    xs = xbc[:, :SSD_INNER]
    b_b = xbc[:, SSD_INNER:SSD_INNER + SSD_BC].astype(BF16)
    c_b = xbc[:, SSD_INNER + SSD_BC:].astype(BF16)

    dt = _softplus(dtr_ref[...] + dtb_ref[...])
    a_cs = _cumsum_rows(dt * (-jnp.exp(alog_ref[...])), C)
    a_end = jnp.concatenate([jnp.broadcast_to(a_cs[(c + 1) * C - 1:(c + 1) * C, :], (C, DT_PAD))
                             for c in range(L // C)], axis=0)
    exp_a = jnp.exp(a_cs)
    to_end = jnp.exp(a_end - a_cs)
    stacked = jnp.concatenate([dt, exp_a, dt * to_end], axis=0)
    hi = stacked.astype(BF16)
    lo = (stacked - hi.astype(F32)).astype(BF16)
    expanded = _dot(jnp.concatenate([hi, lo], axis=1), e_ref[...])
    dt_e, exp_a_e, w_end_e = expanded[:L], expanded[L:2 * L], expanded[2 * L:]

    xdt_b = (xs * dt_e).astype(BF16)
    xend_b = (xs * w_end_e).astype(BF16)
    causal = (lax.broadcasted_iota(jnp.int32, (C, C), 0) >= lax.broadcasted_iota(jnp.int32, (C, C), 1))
    lane = lax.broadcasted_iota(jnp.int32, (C, V7X_LANES), 1)
    heads_per_group = SSD_HEADS // SSD_GROUPS

    y_chunks = []
    for c in range(L // C):
        rows = slice(c * C, (c + 1) * C)
        a_c = a_cs[rows]
        a_c_t = a_c.T
        y_parts = []
        for g in range(SSD_GROUPS):
            gs = slice(g * SSD_STATE, (g + 1) * SSD_STATE)
            gw = slice(g * SSD_GROUP_WIDTH, (g + 1) * SSD_GROUP_WIDTH)
            cb = _dot_nt(c_b[rows, gs], b_b[rows, gs])
            y_off = _dot(c_b[rows, gs], state_ref[g].astype(BF16)) * exp_a_e[rows, gw]
            for pair in range(heads_per_group // 2):
                h0 = g * heads_per_group + 2 * pair
                x_pair = xdt_b[rows, h0 * SSD_HEAD_DIM:(h0 + 2) * SSD_HEAD_DIM]
                zero = jnp.zeros_like(x_pair)
                mixes = []
                for h in (h0, h0 + 1):
                    seg = a_c[:, h:h + 1] - a_c_t[h:h + 1, :]
                    mixes.append((cb * jnp.exp(jnp.where(causal, seg, -jnp.inf))).astype(BF16))
                rhs = jnp.concatenate([jnp.where(lane < SSD_HEAD_DIM, x_pair, zero),
                                       jnp.where(lane >= SSD_HEAD_DIM, x_pair, zero)], axis=0)
                y_pair = _dot(jnp.concatenate(mixes, axis=1), rhs)
                y_parts.append(y_pair + y_off[:, 2 * pair * SSD_HEAD_DIM:(2 * pair + 2) * SSD_HEAD_DIM])
            upd = lax.dot_general(b_b[rows, gs], xend_b[rows, gw], (((0,), (0,)), ((), ())),
                                  preferred_element_type=F32)
            state_ref[g] = state_ref[g] * exp_a_e[(c + 1) * C - 1:(c + 1) * C, gw] + upd
        y_chunks.append(jnp.concatenate(y_parts, axis=1))

    y = jnp.concatenate(y_chunks, axis=0) + xs * dskip_ref[...]
    y = y * _silu(z_ref[...].astype(F32))
    normed = []
    for g in range(SSD_GROUPS):
        yg = y[:, g * SSD_GROUP_WIDTH:(g + 1) * SSD_GROUP_WIDTH]
        normed.append(yg * lax.rsqrt(jnp.mean(yg * yg, axis=-1, keepdims=True) + LN_EPS))
    o_ref[...] = (jnp.concatenate(normed, axis=1) * nw_ref[...]).astype(o_ref.dtype)


def _ssd_mixer(proj, dt_raw, conv_w, conv_b, dt_bias, a_log, d_skip, norm_w, bsz, seq):
    steps = seq // SSD_L
    row = lambda b, s: b * steps + s
    pad16 = lambda v: jnp.pad(v.astype(F32), (0, DT_PAD - SSD_HEADS)).reshape(1, DT_PAD)
    head_of_channel = jnp.arange(SSD_INNER) // SSD_HEAD_DIM
    expand = (jnp.arange(DT_PAD)[:, None] == head_of_channel[None, :]).astype(BF16)
    expand = jnp.concatenate([expand, expand], axis=0)
    t_idx = jnp.arange(SSD_L)
    shifts = jnp.stack([(t_idx[:, None] - d == t_idx[None, :]) for d in range(1, SSD_CONV)]).astype(BF16)
    d_e = jnp.repeat(d_skip.astype(F32), SSD_HEAD_DIM).reshape(1, SSD_INNER)
    const = lambda shape: pl.BlockSpec(shape, lambda b, s: (0,) * len(shape))
    return pl.pallas_call(
        _ssd_kernel, grid=(bsz, steps),
        in_specs=[pl.BlockSpec((SSD_L, SSD_INNER), lambda b, s: (row(b, s), 0)),
                  pl.BlockSpec((SSD_L, SSD_CONV_DIM), lambda b, s: (row(b, s), 2)),
                  pl.BlockSpec((SSD_L, DT_PAD), lambda b, s: (row(b, s), 0)),
                  const((SSD_CONV - 1, SSD_L, SSD_L)),
                  const((SSD_CONV, SSD_CONV_DIM)), const((1, SSD_CONV_DIM)),
                  const((1, DT_PAD)), const((1, DT_PAD)),
                  const((1, SSD_INNER)), const((1, SSD_INNER)), const((2 * DT_PAD, SSD_INNER))],
        out_specs=pl.BlockSpec((SSD_L, SSD_INNER), lambda b, s: (row(b, s), 0)),
        out_shape=jax.ShapeDtypeStruct((bsz * seq, SSD_INNER), BF16),
        scratch_shapes=[pltpu.VMEM((V7X_SUBLANES, SSD_CONV_DIM), F32),
                        pltpu.VMEM((SSD_GROUPS, SSD_STATE, SSD_GROUP_WIDTH), F32)],
        compiler_params=_params("parallel", "arbitrary"), name="ssd_mixer",
    )(proj, proj, dt_raw, shifts, conv_w.astype(F32), conv_b.astype(F32).reshape(1, -1), pad16(dt_bias),
      pad16(a_log), d_e, norm_w.astype(F32).reshape(1, -1), expand)


def _gelu(x):
    return 0.5 * x * (1.0 + lax.erf(x * (1.0 / math.sqrt(2.0))))


def _sgu_kernel(u_ref, v_ref, g_ref, b_ref, w_ref, bs_ref, o_ref):
    v = _layer_norm(_gelu(v_ref[...].astype(F32)), g_ref[...], b_ref[...]).astype(BF16)
    r = lax.broadcasted_iota(jnp.int32, (SGU_BLOCK, SGU_BLOCK), 0) // CHUNK
    c = lax.broadcasted_iota(jnp.int32, (SGU_BLOCK, SGU_BLOCK), 1) // CHUNK
    for g in range(SGU_GROUPS):
        w = jnp.where(r >= c, w_ref[g], 0.0).astype(BF16)
        cols = slice(g * SGU_GROUP_DIM, (g + 1) * SGU_GROUP_DIM)
        for blk in range(SGU_TM // SGU_BLOCK):
            rows = slice(blk * SGU_BLOCK, (blk + 1) * SGU_BLOCK)
            mixed = _dot(w, v[rows, cols]) + bs_ref[g]
            o_ref[rows, cols] = (_gelu(u_ref[rows, cols].astype(F32)) * mixed).astype(o_ref.dtype)


def _sgu_mixer(proj, ln_g, ln_b, w_s, b_s):
    t = proj.shape[0]
    bias = jnp.broadcast_to(b_s.astype(F32)[:, :, None], (SGU_GROUPS, SGU_BLOCK, SGU_GROUP_DIM))
    return pl.pallas_call(
        _sgu_kernel, grid=(t // SGU_TM,),
        in_specs=[pl.BlockSpec((SGU_TM, SGU_WIDTH), lambda i: (i, 1)),
                  pl.BlockSpec((SGU_TM, SGU_WIDTH), lambda i: (i, 2)),
                  pl.BlockSpec((1, SGU_WIDTH), lambda i: (0, 0)), pl.BlockSpec((1, SGU_WIDTH), lambda i: (0, 0)),
                  pl.BlockSpec((SGU_GROUPS, SGU_BLOCK, SGU_BLOCK), lambda i: (0, 0, 0)),
                  pl.BlockSpec((SGU_GROUPS, SGU_BLOCK, SGU_GROUP_DIM), lambda i: (0, 0, 0))],
        out_specs=pl.BlockSpec((SGU_TM, SGU_WIDTH), lambda i: (i, 0)),
        out_shape=jax.ShapeDtypeStruct((t, SGU_WIDTH), BF16),
        compiler_params=_params("parallel"), name="sgu_mixer",
    )(proj, proj, ln_g.astype(F32).reshape(1, -1), ln_b.astype(F32).reshape(1, -1), w_s.astype(F32), bias)


def _attn_kernel(slopes_ref, q_ref, k_ref, v_ref, lq1_ref, lk1_ref, lq2_ref, lk2_ref, sw_ref, o_ref,
                 vt_ref, dbias_ref, kfeat_ref, qq_ref, s0_ref, s1_ref, cmax0_ref, cmax1_ref, m_ref, acc_ref,
                 *, lambda_init):
    t = ATT_T
    s_refs, cmax_refs = (s0_ref, s1_ref), (cmax0_ref, cmax1_ref)
    head, b, i = pl.program_id(0), pl.program_id(1), pl.program_id(2)
    slope = slopes_ref[head]

    @pl.when(i == 0)
    def _():
        ones = jnp.ones((ATT_ONES_ROWS, t), BF16)
        for c in range(vt_ref.shape[0]):
            vt_ref[c, :DIFF_V_DIM, :] = v_ref[c * t:(c + 1) * t, :].astype(F32).T.astype(BF16)
            vt_ref[c, DIFF_V_DIM:, :] = ones
        feat = lax.broadcasted_iota(jnp.int32, (ATT_FEAT, 2 * t), 0)
        qq_ref[2 * DIFF_HEAD_DIM:, :] = jnp.where(feat < 3, 1.0, 0.0).astype(F32).astype(BF16)

    @pl.when((i == 0) & (b == 0))
    def _():
        key = lax.broadcasted_iota(jnp.int32, (t, t), 0)
        qry = lax.broadcasted_iota(jnp.int32, (t, t), 1)
        dbias_ref[...] = jnp.where(key // CHUNK <= qry // CHUNK,
                                   (-2.0 * slope) * jnp.maximum(key - qry, 0).astype(F32), -jnp.inf)
        bias = slope * lax.broadcasted_iota(jnp.int32, (t, ATT_FEAT), 0).astype(F32)
        hi = bias.astype(BF16).astype(F32)
        mid = (bias - hi).astype(BF16).astype(F32)
        lo = bias - hi - mid
        col = lax.broadcasted_iota(jnp.int32, (t, ATT_FEAT), 1)
        kfeat_ref[...] = jnp.where(col == 0, hi, jnp.where(col == 1, mid, jnp.where(col == 2, lo, 0.0))).astype(BF16)

    q_t = q_ref[...].astype(F32).T
    feat = lax.broadcasted_iota(jnp.int32, q_t.shape, 0)
    qq_ref[:2 * DIFF_HEAD_DIM, :t] = jnp.where(feat < DIFF_HEAD_DIM, q_t, 0.0).astype(BF16)
    qq_ref[:2 * DIFF_HEAD_DIM, t:] = jnp.where(feat >= DIFF_HEAD_DIM, q_t, 0.0).astype(BF16)
    m_ref[...] = jnp.full(m_ref.shape, -jnp.inf, F32)
    acc_ref[...] = jnp.zeros(acc_ref.shape, F32)

    def scores(j, slot, diagonal):
        start = pl.multiple_of(j * t, t)
        lhs = jnp.concatenate([k_ref[pl.ds(start, t), :], kfeat_ref[...]], axis=1)
        s = _dot(lhs, qq_ref[...])
        if diagonal:
            dbias = dbias_ref[...]
            s = s + jnp.concatenate([dbias, dbias], axis=1)
        s_refs[slot][...] = s
        cmax_refs[slot][...] = jnp.max(s, axis=0, keepdims=True)

    def accumulate(j, slot):
        shift = slope * ((j - i) * t).astype(F32)
        m = m_ref[...]
        m_new = jnp.maximum(m, cmax_refs[slot][...] + shift)
        alpha = jnp.exp2(m - m_new)
        p = jnp.exp2(s_refs[slot][...] - (m_new - shift))
        m_ref[...] = m_new
        acc_ref[...] = alpha * acc_ref[...] + _dot(vt_ref[j], p.astype(BF16))

    key_tile = lambda n: jnp.where(n == 0, i, n - 1)
    scores(i, 0, diagonal=True)

    def body(pair, carry):
        n = 2 * pair
        scores(key_tile(n + 1), 1, diagonal=False)
        accumulate(key_tile(n), 0)
        scores(key_tile(n + 2), 0, diagonal=False)
        accumulate(key_tile(n + 1), 1)
        return carry

    pairs = i // 2
    lax.fori_loop(0, pairs, body, 0)
    odd = i % 2 == 1

    @pl.when(odd)
    def _():
        scores(i - 1, 1, diagonal=False)

    accumulate(key_tile(2 * pairs), 0)

    @pl.when(odd)
    def _():
        accumulate(i - 1, 1)

    o = acc_ref[:DIFF_V_DIM, :] / acc_ref[DIFF_V_DIM:DIFF_V_DIM + 1, :]
    lam =(jnp.exp(jnp.sum(lq1_ref[...] * lk1_ref[...], axis=-1, keepdims=True))
           - jnp.exp(jnp.sum(lq2_ref[...] * lk2_ref[...], axis=-1, keepdims=True)) + lambda_init)
    o = o[:, :t] - lam * o[:, t:]
    o = o * lax.rsqrt(jnp.mean(o * o, axis=0, keepdims=True) + LN_EPS)
    o_ref[...] = (o.T * sw_ref[...] * (1.0 - lambda_init)).astype(o_ref.dtype)


def _diff_attention(qkv, lq1, lk1, lq2, lk2, subln_w, lambda_init, bsz, seq):
    slopes = LOG2_E * jnp.exp2(-8.0 * jnp.arange(1, DIFF_HEADS + 1, dtype=F32) / DIFF_HEADS)
    vec = lambda p: p.astype(F32).reshape(1, -1)
    head_blk = 2 * DIFF_HEAD_DIM
    small = lambda n: pl.BlockSpec((1, n), lambda h, b, i: (0, 0))
    return pl.pallas_call(
        functools.partial(_attn_kernel, lambda_init=lambda_init),
        grid=(DIFF_HEADS, bsz, seq // ATT_T),
        in_specs=[pl.BlockSpec(memory_space=pltpu.SMEM),
                  pl.BlockSpec((None, ATT_T, head_blk), lambda h, b, i: (b, i, h)),
                  pl.BlockSpec((None, seq, head_blk), lambda h, b, i: (b, 0, DIFF_HEADS + h)),
                  pl.BlockSpec((None, seq, DIFF_V_DIM), lambda h, b, i: (b, 0, 2 * DIFF_HEADS + h)),
                  small(DIFF_HEAD_DIM), small(DIFF_HEAD_DIM), small(DIFF_HEAD_DIM), small(DIFF_HEAD_DIM),
                  small(DIFF_V_DIM)],
        out_specs=pl.BlockSpec((None, ATT_T, DIFF_V_DIM), lambda h, b, i: (b, i, h)),
        out_shape=jax.ShapeDtypeStruct((bsz, seq, DIFF_HEADS * DIFF_V_DIM), BF16),
        scratch_shapes=[pltpu.VMEM((seq // ATT_T, DIFF_V_DIM + ATT_ONES_ROWS, ATT_T), BF16),
                        pltpu.VMEM((ATT_T, ATT_T), F32),
                        pltpu.VMEM((ATT_T, ATT_FEAT), BF16),
                        pltpu.VMEM((DIFF_V_DIM + ATT_FEAT, 2 * ATT_T), BF16),
                        pltpu.VMEM((ATT_T, 2 * ATT_T), F32), pltpu.VMEM((ATT_T, 2 * ATT_T), F32),
                        pltpu.VMEM((1, 2 * ATT_T), F32), pltpu.VMEM((1, 2 * ATT_T), F32),
                        pltpu.VMEM((1, 2 * ATT_T), F32),
                        pltpu.VMEM((DIFF_V_DIM + ATT_ONES_ROWS, 2 * ATT_T), F32)],
        compiler_params=_params("arbitrary", "arbitrary", "arbitrary"), name="diff_attention",
    )(slopes, qkv, qkv, qkv, vec(lq1), vec(lk1), vec(lq2), vec(lk2), vec(subln_w))


def kernel(x, even_w_in, even_conv_w, even_conv_b, even_dt_bias, even_a_log, even_d, even_ssd_norm_w, even_sgu_ln_g, even_sgu_ln_b, even_sgu_w, even_sgu_b, even_w_out, odd_w_qkv, odd_lambda_q1, odd_lambda_k1, odd_lambda_q2, odd_lambda_k2, odd_subln_w, odd_w_out, ln_mix_g, ln_mix_b, ln_ffn_g, ln_ffn_b, mlp_w_up, mlp_w_down):
    bsz, seq, d = x.shape
    h = x.reshape(bsz * seq, d)
    row = lambda p: p.astype(F32).reshape(1, -1)
    for l in range(DEPTH):
        i = l // 2
        if l % 2 == 0:
            w_in = even_w_in[i]
            o_z, o_xbc, o_dt, o_sgu = 0, SSD_INNER, SSD_INNER + SSD_CONV_DIM, SSD_INNER + SSD_CONV_DIM + SSD_HEADS
            w_main = jnp.concatenate([w_in[:, o_z:o_xbc], w_in[:, o_sgu:], w_in[:, o_xbc:o_dt]], axis=1).astype(BF16)
            w_dt = jnp.pad(w_in[:, o_dt:o_sgu], ((0, 0), (0, DT_PAD - SSD_HEADS))).astype(BF16)
            proj, dt_raw = _matmul(h, w_main, PROJ_TM, PROJ_TN, BF16, "in_proj", w_dt=w_dt)
            y_a = _ssd_mixer(proj, dt_raw, even_conv_w[i], even_conv_b[i], even_dt_bias[i], even_a_log[i],
                             even_d[i], even_ssd_norm_w[i], bsz, seq)
            y_b = _sgu_mixer(proj, even_sgu_ln_g[i], even_sgu_ln_b[i], even_sgu_w[i], even_sgu_b[i])
            w_out = even_w_out[i].astype(BF16)
            h = _proj_ln([y_a, y_b], [w_out[:SSD_INNER], w_out[SSD_INNER:]], h, row(ln_mix_g[l]), row(ln_mix_b[l]),
                         "even_out_proj_ln")
        else:
            lambda_init = 0.8 - 0.6 * math.exp(-0.3 * l)
            w_qkv = odd_w_qkv[i]
            q_scale = DIFF_HEAD_DIM ** -0.5 * LOG2_E
            w_qkv = jnp.concatenate([w_qkv[:, :DIFF_QK] * q_scale, w_qkv[:, DIFF_QK:]], axis=1).astype(BF16)
            qkv = _matmul(h, w_qkv, PROJ_TM, PROJ_TN, BF16, "qkv_proj").reshape(bsz, seq, -1)
            o = _diff_attention(qkv, odd_lambda_q1[i], odd_lambda_k1[i], odd_lambda_q2[i], odd_lambda_k2[i],
                                odd_subln_w[i], lambda_init, bsz, seq)
            h = _proj_ln([o.reshape(bsz * seq, -1)], [odd_w_out[i].astype(BF16)], h, row(ln_mix_g[l]), row(ln_mix_b[l]),
                         "odd_out_proj_ln")
        h = _mlp_ln(h, mlp_w_up[l].astype(BF16), mlp_w_down[l].astype(BF16), row(ln_ffn_g[l]), row(ln_ffn_b[l]),
                    "mlp_ln_%d" % l)
    return h.reshape(bsz, seq, d)
```

```python
import functools
import math

import jax
import jax.numpy as jnp
from jax import lax
from jax.experimental import pallas as pl
from jax.experimental.pallas import tpu as pltpu

F32 = jnp.float32
BF16 = jnp.bfloat16

D_MODEL = 1024
DEPTH = 2
CHUNK = 64
SSD_HEADS = 16
SSD_HEAD_DIM = 64
SSD_INNER = SSD_HEADS * SSD_HEAD_DIM
SSD_GROUPS = 2
SSD_STATE = 128
SSD_CONV = 4
SSD_BC = SSD_GROUPS * SSD_STATE
SSD_CONV_DIM = SSD_INNER + 2 * SSD_BC
SSD_GROUP_WIDTH = SSD_INNER // SSD_GROUPS
SGU_BLOCK = 128
SGU_GROUPS = 8
SGU_WIDTH = 1024
SGU_GROUP_DIM = SGU_WIDTH // SGU_GROUPS
DIFF_HEADS = 8
DIFF_HEAD_DIM = 64
DIFF_V_DIM = 2 * DIFF_HEAD_DIM
DIFF_QK = DIFF_HEADS * 2 * DIFF_HEAD_DIM
D_FF = 4 * D_MODEL
DEEPNORM_ALPHA = (2 * DEPTH) ** 0.25
LN_EPS = 1e-5

V7X_LANES = 128
V7X_SUBLANES = 8
V7X_VMEM_BYTES = 64 * 1024 * 1024
VMEM_LIMIT = 56 * 1024 * 1024

PROJ_TM = 1024
PROJ_TN = 1536
LN_TM = 512
MLP_TM = 512
MLP_FF_CHUNK = 1024
SSD_L = 256
SSD_CHUNK = 128
SGU_TM = 512
ATT_T = 512
ATT_ONES_ROWS = 16
ATT_FEAT = V7X_LANES
ATT_TASK_FIELDS = 2
LOG2_E = 1.0 / math.log(2.0)
DT_PAD = V7X_LANES


def _params(*sem):
    return pltpu.CompilerParams(dimension_semantics=sem, vmem_limit_bytes=VMEM_LIMIT)


def _layer_norm(y, g, b):
    mu = jnp.mean(y, axis=-1, keepdims=True)
    yc = y - mu
    var = jnp.mean(yc * yc, axis=-1, keepdims=True)
    return yc * lax.rsqrt(var + LN_EPS) * g + b


def _dot(a, b):
    return jnp.dot(a, b, preferred_element_type=F32)


def _dot_nt(a, b):
    return lax.dot_general(a, b, (((1,), (1,)), ((), ())), preferred_element_type=F32)


def _matmul_kernel(x_ref, w_ref, o_ref):
    o_ref[...] = _dot(x_ref[...].astype(BF16), w_ref[...]).astype(o_ref.dtype)


def _matmul_dt_kernel(x_ref, w_ref, wdt_ref, o_ref, dt_ref):
    xb = x_ref[...].astype(BF16)
    o_ref[...] = _dot(xb, w_ref[...]).astype(o_ref.dtype)

    @pl.when(pl.program_id(1) == 0)
    def _():
        dt_ref[...] = _dot(xb, wdt_ref[...])


def _matmul(x, w, tm, tn, out_dtype, name, w_dt=None):
    t, k = x.shape
    n = w.shape[1]
    grid = (t // tm, n // tn)
    x_spec = pl.BlockSpec((tm, k), lambda i, j: (i, 0))
    w_spec = pl.BlockSpec((k, tn), lambda i, j: (0, j))
    o_spec = pl.BlockSpec((tm, tn), lambda i, j: (i, j))
    if w_dt is None:
        return pl.pallas_call(
            _matmul_kernel, grid=grid, in_specs=[x_spec, w_spec], out_specs=o_spec,
            out_shape=jax.ShapeDtypeStruct((t, n), out_dtype),
            compiler_params=_params("parallel", "arbitrary"), name=name)(x, w)
    return pl.pallas_call(
        _matmul_dt_kernel, grid=grid,
        in_specs=[x_spec, w_spec, pl.BlockSpec((k, DT_PAD), lambda i, j: (0, 0))],
        out_specs=[o_spec, pl.BlockSpec((tm, DT_PAD), lambda i, j: (i, 0))],
        out_shape=[jax.ShapeDtypeStruct((t, n), out_dtype), jax.ShapeDtypeStruct((t, DT_PAD), F32)],
        compiler_params=_params("parallel", "arbitrary"), name=name)(x, w, w_dt)


def _proj_ln_kernel(*refs, n_in):
    x_refs, w_refs = refs[:n_in], refs[n_in:2 * n_in]
    h_ref, g_ref, b_ref, o_ref = refs[2 * n_in:]
    acc = _dot(x_refs[0][...], w_refs[0][...])
    for x_ref, w_ref in zip(x_refs[1:], w_refs[1:]):
        acc += _dot(x_ref[...], w_ref[...])
    o_ref[...] = _layer_norm(DEEPNORM_ALPHA * h_ref[...] + acc, g_ref[...], b_ref[...])


def _proj_ln(xs, ws, h, g, b, name):
    t, d = h.shape
    tm = LN_TM
    n_in = len(xs)
    in_specs = ([pl.BlockSpec((tm, x.shape[1]), lambda i: (i, 0)) for x in xs]
                + [pl.BlockSpec(w.shape, lambda i: (0, 0)) for w in ws]
                + [pl.BlockSpec((tm, d), lambda i: (i, 0)),
                   pl.BlockSpec((1, d), lambda i: (0, 0)), pl.BlockSpec((1, d), lambda i: (0, 0))])
    return pl.pallas_call(
        functools.partial(_proj_ln_kernel, n_in=n_in), grid=(t // tm,), in_specs=in_specs,
        out_specs=pl.BlockSpec((tm, d), lambda i: (i, 0)),
        out_shape=jax.ShapeDtypeStruct((t, d), F32),
        compiler_params=_params("parallel"), name=name)(*xs, *ws, h, g, b)


def _mlp_ln_kernel(h_ref, wu_ref, wd_ref, g_ref, b_ref, o_ref):
    h = h_ref[...]
    hb = h.astype(BF16)
    acc = jnp.zeros(h.shape, F32)
    for c in range(D_FF // MLP_FF_CHUNK):
        cols = slice(c * MLP_FF_CHUNK, (c + 1) * MLP_FF_CHUNK)
        u = jnp.maximum(_dot(hb, wu_ref[:, cols]), 0.0)
        acc += _dot((u * u).astype(BF16), wd_ref[cols, :])
    o_ref[...] = _layer_norm(DEEPNORM_ALPHA * h + acc, g_ref[...], b_ref[...])


def _mlp_ln(h, w_up, w_down, g, b, name):
    t, d = h.shape
    tm = MLP_TM
    return pl.pallas_call(
        _mlp_ln_kernel, grid=(t // tm,),
        in_specs=[pl.BlockSpec((tm, d), lambda i: (i, 0)),
                  pl.BlockSpec(w_up.shape, lambda i: (0, 0)),
                  pl.BlockSpec(w_down.shape, lambda i: (0, 0)),
                  pl.BlockSpec((1, d), lambda i: (0, 0)), pl.BlockSpec((1, d), lambda i: (0, 0))],
        out_specs=pl.BlockSpec((tm, d), lambda i: (i, 0)),
        out_shape=jax.ShapeDtypeStruct((t, d), F32),
        compiler_params=_params("parallel"), name=name)(h, w_up, w_down, g, b)


def _softplus(x):
    return jnp.maximum(x, 0.0) + jnp.log1p(jnp.exp(-jnp.abs(x)))


def _silu(x):
    return x * jax.nn.sigmoid(x)


def _cumsum_rows(x, segment):
    row = lax.broadcasted_iota(jnp.int32, x.shape, 0) % segment
    shift = 1
    while shift < segment:
        x = x + jnp.where(row >= shift, pltpu.roll(x, shift, 0), 0.0)
        shift *= 2
    return x


def _ssd_kernel(z_ref, xbc_ref, dtr_ref, shift_ref, cw_ref, cb_ref, dtb_ref, alog_ref, dskip_ref, nw_ref, e_ref,
                o_ref, tail_ref, state_ref):
    L, C = SSD_L, SSD_CHUNK
    sub = V7X_SUBLANES

    @pl.when(pl.program_id(1) == 0)
    def _():
        tail_ref[...] = jnp.zeros(tail_ref.shape, F32)
        state_ref[...] = jnp.zeros(state_ref.shape, F32)

    x_b = xbc_ref[...]
    x_f = x_b.astype(F32)
    conv = cb_ref[...] + cw_ref[SSD_CONV - 1:SSD_CONV, :] * x_f
    tail = tail_ref[...]
    row8 = lax.broadcasted_iota(jnp.int32, tail.shape, 0)
    head_fix = jnp.zeros(tail.shape, F32)
    for d in range(1, SSD_CONV):
        w = cw_ref[SSD_CONV - 1 - d:SSD_CONV - d, :]
        conv += w * _dot(shift_ref[d - 1], x_b)
        head_fix += w * jnp.where(row8 < d, pltpu.roll(tail, d, 0), 0.0)
    conv = jnp.concatenate([conv[:sub] + head_fix, conv[sub:]], axis=0)
    tail_ref[...] = x_f[L - sub:, :]
    xbc = _silu(conv)
    xs = xbc[:, :SSD_INNER]
    b_b = xbc[:, SSD_INNER:SSD_INNER + SSD_BC].astype(BF16)
    c_b = xbc[:, SSD_INNER + SSD_BC:].astype(BF16)

    dt = _softplus(dtr_ref[...] + dtb_ref[...])
    a_cs = _cumsum_rows(dt * (-jnp.exp(alog_ref[...])), C)
    a_end = jnp.concatenate([jnp.broadcast_to(a_cs[(c + 1) * C - 1:(c + 1) * C, :], (C, DT_PAD))
                             for c in range(L // C)], axis=0)
    exp_a = jnp.exp(a_cs)
    to_end = jnp.exp(a_end - a_cs)
    stacked = jnp.concatenate([dt, exp_a, dt * to_end], axis=0)
    hi = stacked.astype(BF16)
    lo = (stacked - hi.astype(F32)).astype(BF16)
    expanded = _dot(jnp.concatenate([hi, lo], axis=1), e_ref[...])
    dt_e, exp_a_e, w_end_e = expanded[:L], expanded[L:2 * L], expanded[2 * L:]

    xdt_b = (xs * dt_e).astype(BF16)
    xend_b = (xs * w_end_e).astype(BF16)
    causal = (lax.broadcasted_iota(jnp.int32, (C, C), 0) >= lax.broadcasted_iota(jnp.int32, (C, C), 1))
    lane = lax.broadcasted_iota(jnp.int32, (C, V7X_LANES), 1)
    heads_per_group = SSD_HEADS // SSD_GROUPS

    y_chunks = []
    for c in range(L // C):
        rows = slice(c * C, (c + 1) * C)
        a_c = a_cs[rows]
        a_c_t = a_c.T
        y_parts = []
        for g in range(SSD_GROUPS):
            gs = slice(g * SSD_STATE, (g + 1) * SSD_STATE)
            gw = slice(g * SSD_GROUP_WIDTH, (g + 1) * SSD_GROUP_WIDTH)
            cb = _dot_nt(c_b[rows, gs], b_b[rows, gs])
            y_off = _dot(c_b[rows, gs], state_ref[g].astype(BF16)) * exp_a_e[rows, gw]
            for pair in range(heads_per_group // 2):
                h0 = g * heads_per_group + 2 * pair
                x_pair = xdt_b[rows, h0 * SSD_HEAD_DIM:(h0 + 2) * SSD_HEAD_DIM]
                zero = jnp.zeros_like(x_pair)
                mixes = []
                for h in (h0, h0 + 1):
                    seg = a_c[:, h:h + 1] - a_c_t[h:h + 1, :]
                    mixes.append((cb * jnp.exp(jnp.where(causal, seg, -jnp.inf))).astype(BF16))
                rhs = jnp.concatenate([jnp.where(lane < SSD_HEAD_DIM, x_pair, zero),
                                       jnp.where(lane >= SSD_HEAD_DIM, x_pair, zero)], axis=0)
                y_pair = _dot(jnp.concatenate(mixes, axis=1), rhs)
                y_parts.append(y_pair + y_off[:, 2 * pair * SSD_HEAD_DIM:(2 * pair + 2) * SSD_HEAD_DIM])
            upd = lax.dot_general(b_b[rows, gs], xend_b[rows, gw], (((0,), (0,)), ((), ())),
                                  preferred_element_type=F32)
            state_ref[g] = state_ref[g] * exp_a_e[(c + 1) * C - 1:(c + 1) * C, gw] + upd
        y_chunks.append(jnp.concatenate(y_parts, axis=1))

    y = jnp.concatenate(y_chunks, axis=0) + xs * dskip_ref[...]
    y = y * _silu(z_ref[...].astype(F32))
    normed = []
    for g in range(SSD_GROUPS):
        yg = y[:, g * SSD_GROUP_WIDTH:(g + 1) * SSD_GROUP_WIDTH]
        normed.append(yg * lax.rsqrt(jnp.mean(yg * yg, axis=-1, keepdims=True) + LN_EPS))
    o_ref[...] = (jnp.concatenate(normed, axis=1) * nw_ref[...]).astype(o_ref.dtype)


def _ssd_mixer(proj, dt_raw, conv_w, conv_b, dt_bias, a_log, d_skip, norm_w, bsz, seq):
    steps = seq // SSD_L
    row = lambda b, s: b * steps + s
    pad16 = lambda v: jnp.pad(v.astype(F32), (0, DT_PAD - SSD_HEADS)).reshape(1, DT_PAD)
    head_of_channel = jnp.arange(SSD_INNER) // SSD_HEAD_DIM
    expand = (jnp.arange(DT_PAD)[:, None] == head_of_channel[None, :]).astype(BF16)
    expand = jnp.concatenate([expand, expand], axis=0)
    t_idx = jnp.arange(SSD_L)
    shifts = jnp.stack([(t_idx[:, None] - d == t_idx[None, :]) for d in range(1, SSD_CONV)]).astype(BF16)
    d_e = jnp.repeat(d_skip.astype(F32), SSD_HEAD_DIM).reshape(1, SSD_INNER)
    const = lambda shape: pl.BlockSpec(shape, lambda b, s: (0,) * len(shape))
    return pl.pallas_call(
        _ssd_kernel, grid=(bsz, steps),
        in_specs=[pl.BlockSpec((SSD_L, SSD_INNER), lambda b, s: (row(b, s), 0)),
                  pl.BlockSpec((SSD_L, SSD_CONV_DIM), lambda b, s: (row(b, s), 2)),
                  pl.BlockSpec((SSD_L, DT_PAD), lambda b, s: (row(b, s), 0)),
                  const((SSD_CONV - 1, SSD_L, SSD_L)),
                  const((SSD_CONV, SSD_CONV_DIM)), const((1, SSD_CONV_DIM)),
                  const((1, DT_PAD)), const((1, DT_PAD)),
                  const((1, SSD_INNER)), const((1, SSD_INNER)), const((2 * DT_PAD, SSD_INNER))],
        out_specs=pl.BlockSpec((SSD_L, SSD_INNER), lambda b, s: (row(b, s), 0)),
        out_shape=jax.ShapeDtypeStruct((bsz * seq, SSD_INNER), BF16),
        scratch_shapes=[pltpu.VMEM((V7X_SUBLANES, SSD_CONV_DIM), F32),
                        pltpu.VMEM((SSD_GROUPS, SSD_STATE, SSD_GROUP_WIDTH), F32)],
        compiler_params=_params("parallel", "arbitrary"), name="ssd_mixer",
    )(proj, proj, dt_raw, shifts, conv_w.astype(F32), conv_b.astype(F32).reshape(1, -1), pad16(dt_bias),
      pad16(a_log), d_e, norm_w.astype(F32).reshape(1, -1), expand)


def _gelu(x):
    return 0.5 * x * (1.0 + lax.erf(x * (1.0 / math.sqrt(2.0))))


def _sgu_kernel(u_ref, v_ref, g_ref, b_ref, w_ref, bs_ref, o_ref):
    v = _layer_norm(_gelu(v_ref[...].astype(F32)), g_ref[...], b_ref[...]).astype(BF16)
    r = lax.broadcasted_iota(jnp.int32, (SGU_BLOCK, SGU_BLOCK), 0) // CHUNK
    c = lax.broadcasted_iota(jnp.int32, (SGU_BLOCK, SGU_BLOCK), 1) // CHUNK
    for g in range(SGU_GROUPS):
        w = jnp.where(r >= c, w_ref[g], 0.0).astype(BF16)
        cols = slice(g * SGU_GROUP_DIM, (g + 1) * SGU_GROUP_DIM)
        for blk in range(SGU_TM // SGU_BLOCK):
            rows = slice(blk * SGU_BLOCK, (blk + 1) * SGU_BLOCK)
            mixed = _dot(w, v[rows, cols]) + bs_ref[g]
            o_ref[rows, cols] = (_gelu(u_ref[rows, cols].astype(F32)) * mixed).astype(o_ref.dtype)


def _sgu_mixer(proj, ln_g, ln_b, w_s, b_s):
    t = proj.shape[0]
    bias = jnp.broadcast_to(b_s.astype(F32)[:, :, None], (SGU_GROUPS, SGU_BLOCK, SGU_GROUP_DIM))
    return pl.pallas_call(
        _sgu_kernel, grid=(t // SGU_TM,),
        in_specs=[pl.BlockSpec((SGU_TM, SGU_WIDTH), lambda i: (i, 1)),
                  pl.BlockSpec((SGU_TM, SGU_WIDTH), lambda i: (i, 2)),
                  pl.BlockSpec((1, SGU_WIDTH), lambda i: (0, 0)), pl.BlockSpec((1, SGU_WIDTH), lambda i: (0, 0)),
                  pl.BlockSpec((SGU_GROUPS, SGU_BLOCK, SGU_BLOCK), lambda i: (0, 0, 0)),
                  pl.BlockSpec((SGU_GROUPS, SGU_BLOCK, SGU_GROUP_DIM), lambda i: (0, 0, 0))],
        out_specs=pl.BlockSpec((SGU_TM, SGU_WIDTH), lambda i: (i, 0)),
        out_shape=jax.ShapeDtypeStruct((t, SGU_WIDTH), BF16),
        compiler_params=_params("parallel"), name="sgu_mixer",
    )(proj, proj, ln_g.astype(F32).reshape(1, -1), ln_b.astype(F32).reshape(1, -1), w_s.astype(F32), bias)


def _attn_tasks(n_tiles):
    rows = [(i, i) for i in range(n_tiles)] + [(i, j) for i in range(n_tiles) for j in range(i)]
    return len(rows), jnp.asarray(rows, jnp.int32).reshape(-1)


def _attn_kernel(slopes_ref, tasks_ref, q_ref, k_ref, v_ref, lq1_ref, lk1_ref, lq2_ref, lk2_ref, sw_ref, o_ref,
                 vt_ref, dbias_ref, kfeat_ref, qq_ref, s0_ref, s1_ref, cmax0_ref, cmax1_ref, m_ref, acc_ref,
                 *, lambda_init, n_tasks, n_tiles):
    t = ATT_T
    s_refs, cmax_refs = (s0_ref, s1_ref), (cmax0_ref, cmax1_ref)
    head, b = pl.program_id(0), pl.program_id(1)
    slope = slopes_ref[head]
    field = lambda n, f: tasks_ref[ATT_TASK_FIELDS * n + f]

    ones = jnp.ones((ATT_ONES_ROWS, t), BF16)
    for c in range(vt_ref.shape[0]):
        vt_ref[c, :DIFF_V_DIM, :] = v_ref[c * t:(c + 1) * t, :].astype(F32).T.astype(BF16)
        vt_ref[c, DIFF_V_DIM:, :] = ones

    @pl.when(b == 0)
    def _():
        feat = lax.broadcasted_iota(jnp.int32, (ATT_FEAT, 2 * t), 0)
        for i in range(n_tiles):
            qq_ref[i, 2 * DIFF_HEAD_DIM:, :] = jnp.where(feat < 3, 1.0, 0.0).astype(F32).astype(BF16)
        key = lax.broadcasted_iota(jnp.int32, (t, t), 0)
        qry = lax.broadcasted_iota(jnp.int32, (t, t), 1)
        dbias_ref[...] = jnp.where(key // CHUNK <= qry // CHUNK,
                                   (-2.0 * slope) * jnp.maximum(key - qry, 0).astype(F32), -jnp.inf)
        bias = slope * lax.broadcasted_iota(jnp.int32, (t, ATT_FEAT), 0).astype(F32)
        hi = bias.astype(BF16).astype(F32)
        mid = (bias - hi).astype(BF16).astype(F32)
        lo = bias - hi - mid
        col = lax.broadcasted_iota(jnp.int32, (t, ATT_FEAT), 1)
        kfeat_ref[...] = jnp.where(col == 0, hi, jnp.where(col == 1, mid, jnp.where(col == 2, lo, 0.0))).astype(BF16)

    lam = (jnp.exp(jnp.sum(lq1_ref[...] * lk1_ref[...], axis=-1, keepdims=True))
           - jnp.exp(jnp.sum(lq2_ref[...] * lk2_ref[...], axis=-1, keepdims=True)) + lambda_init)

    for i in range(n_tiles):
        q_t = q_ref[i * t:(i + 1) * t, :].astype(F32).T
        feat = lax.broadcasted_iota(jnp.int32, q_t.shape, 0)
        qq_ref[i, :2 * DIFF_HEAD_DIM, :t] = jnp.where(feat < DIFF_HEAD_DIM, q_t, 0.0).astype(BF16)
        qq_ref[i, :2 * DIFF_HEAD_DIM, t:] = jnp.where(feat >= DIFF_HEAD_DIM, q_t, 0.0).astype(BF16)
    m_ref[...] = jnp.full(m_ref.shape, -jnp.inf, F32)
    acc_ref[...] = jnp.zeros(acc_ref.shape, F32)

    def scores(n, slot, diagonal):
        i, j = field(n, 0), field(n, 1)
        lhs = jnp.concatenate([k_ref[pl.ds(pl.multiple_of(j * t, t), t), :], kfeat_ref[...]], axis=1)
        s = _dot(lhs, qq_ref[i])
        if diagonal:
            dbias = dbias_ref[...]
            s = s + jnp.concatenate([dbias, dbias], axis=1)
        s_refs[slot][...] = s
        cmax_refs[slot][...] = jnp.max(s, axis=0, keepdims=True)

    def accumulate(n, slot):
        i, j = field(n, 0), field(n, 1)
        shift = slope * ((j - i) * t).astype(F32)
        m = m_ref[i]
        m_new = jnp.maximum(m, cmax_refs[slot][...] + shift)
        alpha = jnp.exp2(m - m_new)
        p = jnp.exp2(s_refs[slot][...] - (m_new - shift))
        m_ref[i] = m_new
        acc_ref[i] = alpha * acc_ref[i] + _dot(vt_ref[j], p.astype(BF16))

    def stage(n, diagonal):
        scores(n + 1, 1 - n % 2, diagonal)
        accumulate(n, n % 2)

    def run_stages(first, stop, diagonal):
        if first < stop and first % 2 == 1:
            stage(first, diagonal)
            first += 1
        pairs = (stop - first) // 2

        def body(r, carry):
            n = first + 2 * r
            scores(n + 1, 1, diagonal)
            accumulate(n, 0)
            scores(n + 2, 0, diagonal)
            accumulate(n + 1, 1)
            return carry

        if pairs > 0:
            lax.fori_loop(0, pairs, body, 0)
        for n in range(first + 2 * pairs, stop):
            stage(n, diagonal)

    scores(0, 0, diagonal=True)
    run_stages(0, n_tiles - 1, diagonal=True)
    run_stages(n_tiles - 1, n_tasks - 1, diagonal=False)
    accumulate(n_tasks - 1, (n_tasks - 1) % 2)

    for i in range(n_tiles):
        o = acc_ref[i, :DIFF_V_DIM, :] / acc_ref[i, DIFF_V_DIM:DIFF_V_DIM + 1, :]
        o = o[:, :t] - lam * o[:, t:]
        o = o * lax.rsqrt(jnp.mean(o * o, axis=0, keepdims=True) + LN_EPS)
        o_ref[i * t:(i + 1) * t, :] = (o.T * sw_ref[...] * (1.0 - lambda_init)).astype(o_ref.dtype)


def _diff_attention(qkv, lq1, lk1, lq2, lk2, subln_w, lambda_init, bsz, seq):
    slopes = LOG2_E * jnp.exp2(-8.0 * jnp.arange(1, DIFF_HEADS + 1, dtype=F32) / DIFF_HEADS)
    n_tiles = seq // ATT_T
    n_tasks, tasks = _attn_tasks(n_tiles)
    vec = lambda p: p.astype(F32).reshape(1, -1)
    head_blk = 2 * DIFF_HEAD_DIM
    small = lambda n: pl.BlockSpec((1, n), lambda h, b: (0, 0))
    smem = pl.BlockSpec(memory_space=pltpu.SMEM)
    return pl.pallas_call(
        functools.partial(_attn_kernel, lambda_init=lambda_init, n_tasks=n_tasks, n_tiles=n_tiles),
        grid=(DIFF_HEADS, bsz),
        in_specs=[smem, smem,
                  pl.BlockSpec((None, seq, head_blk), lambda h, b: (b, 0, h)),
                  pl.BlockSpec((None, seq, head_blk), lambda h, b: (b, 0, DIFF_HEADS + h)),
                  pl.BlockSpec((None, seq, DIFF_V_DIM), lambda h, b: (b, 0, 2 * DIFF_HEADS + h)),
                  small(DIFF_HEAD_DIM), small(DIFF_HEAD_DIM), small(DIFF_HEAD_DIM), small(DIFF_HEAD_DIM),
                  small(DIFF_V_DIM)],
        out_specs=pl.BlockSpec((None, seq, DIFF_V_DIM), lambda h, b: (b, 0, h)),
        out_shape=jax.ShapeDtypeStruct((bsz, seq, DIFF_HEADS * DIFF_V_DIM), BF16),
        scratch_shapes=[pltpu.VMEM((n_tiles, DIFF_V_DIM + ATT_ONES_ROWS, ATT_T), BF16),
                        pltpu.VMEM((ATT_T, ATT_T), F32),
                        pltpu.VMEM((ATT_T, ATT_FEAT), BF16),
                        pltpu.VMEM((n_tiles, DIFF_V_DIM + ATT_FEAT, 2 * ATT_T), BF16),
                        pltpu.VMEM((ATT_T, 2 * ATT_T), F32), pltpu.VMEM((ATT_T, 2 * ATT_T), F32),
                        pltpu.VMEM((1, 2 * ATT_T), F32), pltpu.VMEM((1, 2 * ATT_T), F32),
                        pltpu.VMEM((n_tiles, 1, 2 * ATT_T), F32),
                        pltpu.VMEM((n_tiles, DIFF_V_DIM + ATT_ONES_ROWS, 2 * ATT_T), F32)],
        compiler_params=_params("arbitrary", "arbitrary"), name="diff_attention",
    )(slopes, tasks, qkv, qkv, qkv, vec(lq1), vec(lk1), vec(lq2), vec(lk2), vec(subln_w))


def kernel(x, even_w_in, even_conv_w, even_conv_b, even_dt_bias, even_a_log, even_d, even_ssd_norm_w, even_sgu_ln_g, even_sgu_ln_b, even_sgu_w, even_sgu_b, even_w_out, odd_w_qkv, odd_lambda_q1, odd_lambda_k1, odd_lambda_q2, odd_lambda_k2, odd_subln_w, odd_w_out, ln_mix_g, ln_mix_b, ln_ffn_g, ln_ffn_b, mlp_w_up, mlp_w_down):
    bsz, seq, d = x.shape
    h = x.reshape(bsz * seq, d)
    row = lambda p: p.astype(F32).reshape(1, -1)
    for l in range(DEPTH):
        i = l // 2
        if l % 2 == 0:
            w_in = even_w_in[i]
            o_z, o_xbc, o_dt, o_sgu = 0, SSD_INNER, SSD_INNER + SSD_CONV_DIM, SSD_INNER + SSD_CONV_DIM + SSD_HEADS
            w_main = jnp.concatenate([w_in[:, o_z:o_xbc], w_in[:, o_sgu:], w_in[:, o_xbc:o_dt]], axis=1).astype(BF16)
            w_dt = jnp.pad(w_in[:, o_dt:o_sgu], ((0, 0), (0, DT_PAD - SSD_HEADS))).astype(BF16)
            proj, dt_raw = _matmul(h, w_main, PROJ_TM, PROJ_TN, BF16, "in_proj", w_dt=w_dt)
            y_a = _ssd_mixer(proj, dt_raw, even_conv_w[i], even_conv_b[i], even_dt_bias[i], even_a_log[i],
                             even_d[i], even_ssd_norm_w[i], bsz, seq)
            y_b = _sgu_mixer(proj, even_sgu_ln_g[i], even_sgu_ln_b[i], even_sgu_w[i], even_sgu_b[i])
            w_out = even_w_out[i].astype(BF16)
            h = _proj_ln([y_a, y_b], [w_out[:SSD_INNER], w_out[SSD_INNER:]], h, row(ln_mix_g[l]), row(ln_mix_b[l]),
                         "even_out_proj_ln")
        else:
            lambda_init = 0.8 - 0.6 * math.exp(-0.3 * l)
            w_qkv = odd_w_qkv[i]
            q_scale = DIFF_HEAD_DIM ** -0.5 * LOG2_E
            w_qkv = jnp.concatenate([w_qkv[:, :DIFF_QK] * q_scale, w_qkv[:, DIFF_QK:]], axis=1).astype(BF16)
            qkv = _matmul(h, w_qkv, PROJ_TM, PROJ_TN, BF16, "qkv_proj").reshape(bsz, seq, -1)
            o = _diff_attention(qkv, odd_lambda_q1[i], odd_lambda_k1[i], odd_lambda_q2[i], odd_lambda_k2[i],
                                odd_subln_w[i], lambda_init, bsz, seq)
            h = _proj_ln([o.reshape(bsz * seq, -1)], [odd_w_out[i].astype(BF16)], h, row(ln_mix_g[l]), row(ln_mix_b[l]),
                         "odd_out_proj_ln")
        h = _mlp_ln(h, mlp_w_up[l].astype(BF16), mlp_w_down[l].astype(BF16), row(ln_ffn_g[l]), row(ln_ffn_b[l]),
                    "mlp_ln_%d" % l)
    return h.reshape(bsz, seq, d)
```

```python
import functools
import math

import jax
import jax.numpy as jnp
from jax import lax
from jax.experimental import pallas as pl
from jax.experimental.pallas import tpu as pltpu

F32 = jnp.float32
BF16 = jnp.bfloat16

D_MODEL = 1024
DEPTH = 2
CHUNK = 64
SSD_HEADS = 16
SSD_HEAD_DIM = 64
SSD_INNER = SSD_HEADS * SSD_HEAD_DIM
SSD_GROUPS = 2
SSD_STATE = 128
SSD_CONV = 4
SSD_BC = SSD_GROUPS * SSD_STATE
SSD_CONV_DIM = SSD_INNER + 2 * SSD_BC
SSD_GROUP_WIDTH = SSD_INNER // SSD_GROUPS
SGU_BLOCK = 128
SGU_GROUPS = 8
SGU_WIDTH = 1024
SGU_GROUP_DIM = SGU_WIDTH // SGU_GROUPS
DIFF_HEADS = 8
DIFF_HEAD_DIM = 64
DIFF_V_DIM = 2 * DIFF_HEAD_DIM
DIFF_QK = DIFF_HEADS * 2 * DIFF_HEAD_DIM
D_FF = 4 * D_MODEL
DEEPNORM_ALPHA = (2 * DEPTH) ** 0.25
LN_EPS = 1e-5

V7X_LANES = 128
V7X_SUBLANES = 8
V7X_VMEM_BYTES = 64 * 1024 * 1024
VMEM_LIMIT = 56 * 1024 * 1024

PROJ_TM = 1024
PROJ_TN = 1536
LN_TM = 512
MLP_TM = 512
MLP_FF_CHUNK = 1024
SSD_L = 256
SSD_CHUNK = 128
SGU_TM = 512
ATT_T = 512
ATT_ONES_ROWS = 16
ATT_FEAT = V7X_LANES
ATT_TASK_FIELDS = 2
ATT_UNROLL = 4
LOG2_E = 1.0 / math.log(2.0)
DT_PAD = V7X_LANES


def _params(*sem):
    return pltpu.CompilerParams(dimension_semantics=sem, vmem_limit_bytes=VMEM_LIMIT)


def _layer_norm(y, g, b):
    mu = jnp.mean(y, axis=-1, keepdims=True)
    yc = y - mu
    var = jnp.mean(yc * yc, axis=-1, keepdims=True)
    return yc * lax.rsqrt(var + LN_EPS) * g + b


def _dot(a, b):
    return jnp.dot(a, b, preferred_element_type=F32)


def _dot_nt(a, b):
    return lax.dot_general(a, b, (((1,), (1,)), ((), ())), preferred_element_type=F32)


def _matmul_kernel(x_ref, w_ref, o_ref):
    o_ref[...] = _dot(x_ref[...].astype(BF16), w_ref[...]).astype(o_ref.dtype)


def _matmul_dt_kernel(x_ref, w_ref, wdt_ref, o_ref, dt_ref):
    xb = x_ref[...].astype(BF16)
    o_ref[...] = _dot(xb, w_ref[...]).astype(o_ref.dtype)

    @pl.when(pl.program_id(1) == 0)
    def _():
        dt_ref[...] = _dot(xb, wdt_ref[...])


def _matmul(x, w, tm, tn, out_dtype, name, w_dt=None):
    t, k = x.shape
    n = w.shape[1]
    grid = (t // tm, n // tn)
    x_spec = pl.BlockSpec((tm, k), lambda i, j: (i, 0))
    w_spec = pl.BlockSpec((k, tn), lambda i, j: (0, j))
    o_spec = pl.BlockSpec((tm, tn), lambda i, j: (i, j))
    if w_dt is None:
        return pl.pallas_call(
            _matmul_kernel, grid=grid, in_specs=[x_spec, w_spec], out_specs=o_spec,
            out_shape=jax.ShapeDtypeStruct((t, n), out_dtype),
            compiler_params=_params("parallel", "arbitrary"), name=name)(x, w)
    return pl.pallas_call(
        _matmul_dt_kernel, grid=grid,
        in_specs=[x_spec, w_spec, pl.BlockSpec((k, DT_PAD), lambda i, j: (0, 0))],
        out_specs=[o_spec, pl.BlockSpec((tm, DT_PAD), lambda i, j: (i, 0))],
        out_shape=[jax.ShapeDtypeStruct((t, n), out_dtype), jax.ShapeDtypeStruct((t, DT_PAD), F32)],
        compiler_params=_params("parallel", "arbitrary"), name=name)(x, w, w_dt)


def _proj_ln_kernel(*refs, n_in):
    x_refs, w_refs = refs[:n_in], refs[n_in:2 * n_in]
    h_ref, g_ref, b_ref, o_ref = refs[2 * n_in:]
    acc = _dot(x_refs[0][...], w_refs[0][...])
    for x_ref, w_ref in zip(x_refs[1:], w_refs[1:]):
        acc += _dot(x_ref[...], w_ref[...])
    o_ref[...] = _layer_norm(DEEPNORM_ALPHA * h_ref[...] + acc, g_ref[...], b_ref[...])


def _proj_ln(xs, ws, h, g, b, name):
    t, d = h.shape
    tm = LN_TM
    n_in = len(xs)
    in_specs = ([pl.BlockSpec((tm, x.shape[1]), lambda i: (i, 0)) for x in xs]
                + [pl.BlockSpec(w.shape, lambda i: (0, 0)) for w in ws]
                + [pl.BlockSpec((tm, d), lambda i: (i, 0)),
                   pl.BlockSpec((1, d), lambda i: (0, 0)), pl.BlockSpec((1, d), lambda i: (0, 0))])
    return pl.pallas_call(
        functools.partial(_proj_ln_kernel, n_in=n_in), grid=(t // tm,), in_specs=in_specs,
        out_specs=pl.BlockSpec((tm, d), lambda i: (i, 0)),
        out_shape=jax.ShapeDtypeStruct((t, d), F32),
        compiler_params=_params("parallel"), name=name)(*xs, *ws, h, g, b)


def _mlp_ln_kernel(h_ref, wu_ref, wd_ref, g_ref, b_ref, o_ref):
    h = h_ref[...]
    hb = h.astype(BF16)
    acc = jnp.zeros(h.shape, F32)
    for c in range(D_FF // MLP_FF_CHUNK):
        cols = slice(c * MLP_FF_CHUNK, (c + 1) * MLP_FF_CHUNK)
        u = jnp.maximum(_dot(hb, wu_ref[:, cols]), 0.0)
        acc += _dot((u * u).astype(BF16), wd_ref[cols, :])
    o_ref[...] = _layer_norm(DEEPNORM_ALPHA * h + acc, g_ref[...], b_ref[...])


def _mlp_ln(h, w_up, w_down, g, b, name):
    t, d = h.shape
    tm = MLP_TM
    return pl.pallas_call(
        _mlp_ln_kernel, grid=(t // tm,),
        in_specs=[pl.BlockSpec((tm, d), lambda i: (i, 0)),
                  pl.BlockSpec(w_up.shape, lambda i: (0, 0)),
                  pl.BlockSpec(w_down.shape, lambda i: (0, 0)),
                  pl.BlockSpec((1, d), lambda i: (0, 0)), pl.BlockSpec((1, d), lambda i: (0, 0))],
        out_specs=pl.BlockSpec((tm, d), lambda i: (i, 0)),
        out_shape=jax.ShapeDtypeStruct((t, d), F32),
        compiler_params=_params("parallel"), name=name)(h, w_up, w_down, g, b)


def _softplus(x):
    return jnp.maximum(x, 0.0) + jnp.log1p(jnp.exp(-jnp.abs(x)))


def _silu(x):
    h = 0.5 * x
    return h + h * jnp.tanh(h)


def _cumsum_rows(x, segment):
    row = lax.broadcasted_iota(jnp.int32, x.shape, 0) % segment
    shift = 1
    while shift < segment:
        x = x + jnp.where(row >= shift, pltpu.roll(x, shift, 0), 0.0)
        shift *= 2
    return x


def _ssd_kernel(z_ref, xbc_ref, dtr_ref, shift_ref, cw_ref, cb_ref, dtb_ref, alog_ref, dskip_ref, nw_ref, e_ref,
                o_ref, tail_ref, state_ref):
    L, C = SSD_L, SSD_CHUNK
    sub = V7X_SUBLANES

    @pl.when(pl.program_id(1) == 0)
    def _():
        tail_ref[...] = jnp.zeros(tail_ref.shape, F32)
        state_ref[...] = jnp.zeros(state_ref.shape, F32)

    x_b = xbc_ref[...]
    x_f = x_b.astype(F32)
    conv = cb_ref[...] + cw_ref[SSD_CONV - 1:SSD_CONV, :] * x_f
    tail = tail_ref[...]
    row8 = lax.broadcasted_iota(jnp.int32, tail.shape, 0)
    head_fix = jnp.zeros(tail.shape, F32)
    for d in range(1, SSD_CONV):
        w = cw_ref[SSD_CONV - 1 - d:SSD_CONV - d, :]
        conv += w * _dot(shift_ref[d - 1], x_b)
        head_fix += w * jnp.where(row8 < d, pltpu.roll(tail, d, 0), 0.0)
    conv = jnp.concatenate([conv[:sub] + head_fix, conv[sub:]], axis=0)
    tail_ref[...] = x_f[L - sub:, :]
    xbc = _silu(conv)
    xs = xbc[:, :SSD_INNER]
    b_b = xbc[:, SSD_INNER:SSD_INNER + SSD_BC].astype(BF16)
    c_b = xbc[:, SSD_INNER + SSD_BC:].astype(BF16)

    dt = _softplus(dtr_ref[...] + dtb_ref[...])
    a_cs = _cumsum_rows(dt * (-LOG2_E * jnp.exp(alog_ref[...])), C)
    a_end = jnp.concatenate([jnp.broadcast_to(a_cs[(c + 1) * C - 1:(c + 1) * C, :], (C, DT_PAD))
                             for c in range(L // C)], axis=0)
    exp_a = jnp.exp2(a_cs)
    to_end = jnp.exp2(a_end - a_cs)
    stacked = jnp.concatenate([dt, exp_a, dt * to_end], axis=0)
    hi = stacked.astype(BF16)
    lo = (stacked - hi.astype(F32)).astype(BF16)
    expanded = _dot(jnp.concatenate([hi, lo], axis=1), e_ref[...])
    dt_e, exp_a_e, w_end_e = expanded[:L], expanded[L:2 * L], expanded[2 * L:]

    xdt_b = (xs * dt_e).astype(BF16)
    xend_b = (xs * w_end_e).astype(BF16)
    causal = (lax.broadcasted_iota(jnp.int32, (C, C), 0) >= lax.broadcasted_iota(jnp.int32, (C, C), 1))
    lane = lax.broadcasted_iota(jnp.int32, (C, V7X_LANES), 1)
    heads_per_group = SSD_HEADS // SSD_GROUPS

    y_chunks = []
    for c in range(L // C):
        rows = slice(c * C, (c + 1) * C)
        a_c = a_cs[rows]
        a_c_t = a_c.T
        y_parts = []
        for g in range(SSD_GROUPS):
            gs = slice(g * SSD_STATE, (g + 1) * SSD_STATE)
            gw = slice(g * SSD_GROUP_WIDTH, (g + 1) * SSD_GROUP_WIDTH)
            cb = _dot_nt(c_b[rows, gs], b_b[rows, gs])
            y_off = _dot(c_b[rows, gs], state_ref[g].astype(BF16)) * exp_a_e[rows, gw]
            for pair in range(heads_per_group // 2):
                h0 = g * heads_per_group + 2 * pair
                x_pair = xdt_b[rows, h0 * SSD_HEAD_DIM:(h0 + 2) * SSD_HEAD_DIM]
                zero = jnp.zeros_like(x_pair)
                mixes = []
                for h in (h0, h0 + 1):
                    seg = a_c[:, h:h + 1] - a_c_t[h:h + 1, :]
                    mixes.append((cb * jnp.exp2(jnp.where(causal, seg, -jnp.inf))).astype(BF16))
                rhs = jnp.concatenate([jnp.where(lane < SSD_HEAD_DIM, x_pair, zero),
                                       jnp.where(lane >= SSD_HEAD_DIM, x_pair, zero)], axis=0)
                y_pair = _dot(jnp.concatenate(mixes, axis=1), rhs)
                y_parts.append(y_pair + y_off[:, 2 * pair * SSD_HEAD_DIM:(2 * pair + 2) * SSD_HEAD_DIM])
            upd = lax.dot_general(b_b[rows, gs], xend_b[rows, gw], (((0,), (0,)), ((), ())),
                                  preferred_element_type=F32)
            state_ref[g] = state_ref[g] * exp_a_e[(c + 1) * C - 1:(c + 1) * C, gw] + upd
        y_chunks.append(jnp.concatenate(y_parts, axis=1))

    y = jnp.concatenate(y_chunks, axis=0) + xs * dskip_ref[...]
    y = y * _silu(z_ref[...].astype(F32))
    normed = []
    for g in range(SSD_GROUPS):
        yg = y[:, g * SSD_GROUP_WIDTH:(g + 1) * SSD_GROUP_WIDTH]
        normed.append(yg * lax.rsqrt(jnp.mean(yg * yg, axis=-1, keepdims=True) + LN_EPS))
    o_ref[...] = (jnp.concatenate(normed, axis=1) * nw_ref[...]).astype(o_ref.dtype)


def _ssd_mixer(proj, dt_raw, conv_w, conv_b, dt_bias, a_log, d_skip, norm_w, bsz, seq):
    steps = seq // SSD_L
    row = lambda b, s: b * steps + s
    pad16 = lambda v: jnp.pad(v.astype(F32), (0, DT_PAD - SSD_HEADS)).reshape(1, DT_PAD)
    head_of_channel = jnp.arange(SSD_INNER) // SSD_HEAD_DIM
    expand = (jnp.arange(DT_PAD)[:, None] == head_of_channel[None, :]).astype(BF16)
    expand = jnp.concatenate([expand, expand], axis=0)
    t_idx = jnp.arange(SSD_L)
    shifts = jnp.stack([(t_idx[:, None] - d == t_idx[None, :]) for d in range(1, SSD_CONV)]).astype(BF16)
    d_e = jnp.repeat(d_skip.astype(F32), SSD_HEAD_DIM).reshape(1, SSD_INNER)
    const = lambda shape: pl.BlockSpec(shape, lambda b, s: (0,) * len(shape))
    return pl.pallas_call(
        _ssd_kernel, grid=(bsz, steps),
        in_specs=[pl.BlockSpec((SSD_L, SSD_INNER), lambda b, s: (row(b, s), 0)),
                  pl.BlockSpec((SSD_L, SSD_CONV_DIM), lambda b, s: (row(b, s), 2)),
                  pl.BlockSpec((SSD_L, DT_PAD), lambda b, s: (row(b, s), 0)),
                  const((SSD_CONV - 1, SSD_L, SSD_L)),
                  const((SSD_CONV, SSD_CONV_DIM)), const((1, SSD_CONV_DIM)),
                  const((1, DT_PAD)), const((1, DT_PAD)),
                  const((1, SSD_INNER)), const((1, SSD_INNER)), const((2 * DT_PAD, SSD_INNER))],
        out_specs=pl.BlockSpec((SSD_L, SSD_INNER), lambda b, s: (row(b, s), 0)),
        out_shape=jax.ShapeDtypeStruct((bsz * seq, SSD_INNER), BF16),
        scratch_shapes=[pltpu.VMEM((V7X_SUBLANES, SSD_CONV_DIM), F32),
                        pltpu.VMEM((SSD_GROUPS, SSD_STATE, SSD_GROUP_WIDTH), F32)],
        compiler_params=_params("parallel", "arbitrary"), name="ssd_mixer",
    )(proj, proj, dt_raw, shifts, conv_w.astype(F32), conv_b.astype(F32).reshape(1, -1), pad16(dt_bias),
      pad16(a_log), d_e, norm_w.astype(F32).reshape(1, -1), expand)


def _gelu(x):
    return 0.5 * x * (1.0 + lax.erf(x * (1.0 / math.sqrt(2.0))))


def _sgu_kernel(u_ref, v_ref, g_ref, b_ref, w_ref, bs_ref, o_ref):
    v = _layer_norm(_gelu(v_ref[...].astype(F32)), g_ref[...], b_ref[...]).astype(BF16)
    r = lax.broadcasted_iota(jnp.int32, (SGU_BLOCK, SGU_BLOCK), 0) // CHUNK
    c = lax.broadcasted_iota(jnp.int32, (SGU_BLOCK, SGU_BLOCK), 1) // CHUNK
    for g in range(SGU_GROUPS):
        w = jnp.where(r >= c, w_ref[g], 0.0).astype(BF16)
        cols = slice(g * SGU_GROUP_DIM, (g + 1) * SGU_GROUP_DIM)
        for blk in range(SGU_TM // SGU_BLOCK):
            rows = slice(blk * SGU_BLOCK, (blk + 1) * SGU_BLOCK)
            mixed = _dot(w, v[rows, cols]) + bs_ref[g]
            o_ref[rows, cols] = (_gelu(u_ref[rows, cols].astype(F32)) * mixed).astype(o_ref.dtype)


def _sgu_mixer(proj, ln_g, ln_b, w_s, b_s):
    t = proj.shape[0]
    bias = jnp.broadcast_to(b_s.astype(F32)[:, :, None], (SGU_GROUPS, SGU_BLOCK, SGU_GROUP_DIM))
    return pl.pallas_call(
        _sgu_kernel, grid=(t // SGU_TM,),
        in_specs=[pl.BlockSpec((SGU_TM, SGU_WIDTH), lambda i: (i, 1)),
                  pl.BlockSpec((SGU_TM, SGU_WIDTH), lambda i: (i, 2)),
                  pl.BlockSpec((1, SGU_WIDTH), lambda i: (0, 0)), pl.BlockSpec((1, SGU_WIDTH), lambda i: (0, 0)),
                  pl.BlockSpec((SGU_GROUPS, SGU_BLOCK, SGU_BLOCK), lambda i: (0, 0, 0)),
                  pl.BlockSpec((SGU_GROUPS, SGU_BLOCK, SGU_GROUP_DIM), lambda i: (0, 0, 0))],
        out_specs=pl.BlockSpec((SGU_TM, SGU_WIDTH), lambda i: (i, 0)),
        out_shape=jax.ShapeDtypeStruct((t, SGU_WIDTH), BF16),
        compiler_params=_params("parallel"), name="sgu_mixer",
    )(proj, proj, ln_g.astype(F32).reshape(1, -1), ln_b.astype(F32).reshape(1, -1), w_s.astype(F32), bias)


def _attn_tasks(n_tiles):
    rows = [(i, i) for i in range(n_tiles)] + [(i, j) for i in range(n_tiles) for j in range(i)]
    return len(rows), jnp.asarray(rows, jnp.int32).reshape(-1)


def _attn_kernel(slopes_ref, tasks_ref, q_ref, k_ref, v_ref, lq1_ref, lk1_ref, lq2_ref, lk2_ref, sw_ref, o_ref,
                 vt_ref, dbias_ref, kfeat_ref, qq_ref, s0_ref, s1_ref, cmax0_ref, cmax1_ref, m_ref, acc_ref,
                 *, lambda_init, n_tasks, n_tiles):
    t = ATT_T
    s_refs, cmax_refs = (s0_ref, s1_ref), (cmax0_ref, cmax1_ref)
    head, b = pl.program_id(0), pl.program_id(1)
    slope = slopes_ref[head]
    field = lambda n, f: tasks_ref[ATT_TASK_FIELDS * n + f]

    ones = jnp.ones((ATT_ONES_ROWS, t), BF16)
    for c in range(vt_ref.shape[0]):
        vt_ref[c, :DIFF_V_DIM, :] = v_ref[c * t:(c + 1) * t, :].astype(F32).T.astype(BF16)
        vt_ref[c, DIFF_V_DIM:, :] = ones

    @pl.when(b == 0)
    def _():
        feat = lax.broadcasted_iota(jnp.int32, (ATT_FEAT, 2 * t), 0)
        for i in range(n_tiles):
            qq_ref[i, 2 * DIFF_HEAD_DIM:, :] = jnp.where(feat < 3, 1.0, 0.0).astype(F32).astype(BF16)
        key = lax.broadcasted_iota(jnp.int32, (t, t), 0)
        qry = lax.broadcasted_iota(jnp.int32, (t, t), 1)
        dbias_ref[...] = jnp.where(key // CHUNK <= qry // CHUNK,
                                   (-2.0 * slope) * jnp.maximum(key - qry, 0).astype(F32), -jnp.inf)
        bias = slope * lax.broadcasted_iota(jnp.int32, (t, ATT_FEAT), 0).astype(F32)
        hi = bias.astype(BF16).astype(F32)
        mid = (bias - hi).astype(BF16).astype(F32)
        lo = bias - hi - mid
        col = lax.broadcasted_iota(jnp.int32, (t, ATT_FEAT), 1)
        kfeat_ref[...] = jnp.where(col == 0, hi, jnp.where(col == 1, mid, jnp.where(col == 2, lo, 0.0))).astype(BF16)

    lam = (jnp.exp(jnp.sum(lq1_ref[...] * lk1_ref[...], axis=-1, keepdims=True))
           - jnp.exp(jnp.sum(lq2_ref[...] * lk2_ref[...], axis=-1, keepdims=True)) + lambda_init)

    for i in range(n_tiles):
        q_t = q_ref[i * t:(i + 1) * t, :].astype(F32).T
        feat = lax.broadcasted_iota(jnp.int32, q_t.shape, 0)
        qq_ref[i, :2 * DIFF_HEAD_DIM, :t] = jnp.where(feat < DIFF_HEAD_DIM, q_t, 0.0).astype(BF16)
        qq_ref[i, :2 * DIFF_HEAD_DIM, t:] = jnp.where(feat >= DIFF_HEAD_DIM, q_t, 0.0).astype(BF16)
    m_ref[...] = jnp.full(m_ref.shape, -jnp.inf, F32)
    acc_ref[...] = jnp.zeros(acc_ref.shape, F32)

    def scores(n, slot, diagonal):
        i, j = field(n, 0), field(n, 1)
        lhs = jnp.concatenate([k_ref[pl.ds(pl.multiple_of(j * t, t), t), :], kfeat_ref[...]], axis=1)
        s = _dot(lhs, qq_ref[i])
        if diagonal:
            dbias = dbias_ref[...]
            s = s + jnp.concatenate([dbias, dbias], axis=1)
        s_refs[slot][...] = s
        cmax_refs[slot][...] = jnp.max(s, axis=0, keepdims=True)

    def accumulate(n, slot):
        i, j = field(n, 0), field(n, 1)
        shift = slope * ((j - i) * t).astype(F32)
        m = m_ref[i]
        m_new = jnp.maximum(m, cmax_refs[slot][...] + shift)
        alpha = jnp.exp2(m - m_new)
        p = jnp.exp2(s_refs[slot][...] - (m_new - shift))
        m_ref[i] = m_new
        acc_ref[i] = alpha * acc_ref[i] + _dot(vt_ref[j], p.astype(BF16))

    def stage(n, diagonal):
        scores(n + 1, 1 - n % 2, diagonal)
        accumulate(n, n % 2)

    def run_stages(first, stop, diagonal):
        if first < stop and first % 2 == 1:
            stage(first, diagonal)
            first += 1
        rounds = (stop - first) // ATT_UNROLL

        def body(r, carry):
            n = first + ATT_UNROLL * r
            for u in range(ATT_UNROLL):
                scores(n + u + 1, (u + 1) % 2, diagonal)
                accumulate(n + u, u % 2)
            return carry

        if rounds > 0:
            lax.fori_loop(0, rounds, body, 0)
        for n in range(first + ATT_UNROLL * rounds, stop):
            stage(n, diagonal)

    scores(0, 0, diagonal=True)
    run_stages(0, n_tiles - 1, diagonal=True)
    run_stages(n_tiles - 1, n_tasks - 1, diagonal=False)
    accumulate(n_tasks - 1, (n_tasks - 1) % 2)

    for i in range(n_tiles):
        o = acc_ref[i, :DIFF_V_DIM, :] * (1.0 / acc_ref[i, DIFF_V_DIM:DIFF_V_DIM + 1, :])
        o = o[:, :t] - lam * o[:, t:]
        o = o * lax.rsqrt(jnp.mean(o * o, axis=0, keepdims=True) + LN_EPS)
        o_ref[i * t:(i + 1) * t, :] = (o.T * sw_ref[...] * (1.0 - lambda_init)).astype(o_ref.dtype)


def _diff_attention(qkv, lq1, lk1, lq2, lk2, subln_w, lambda_init, bsz, seq):
    slopes = LOG2_E * jnp.exp2(-8.0 * jnp.arange(1, DIFF_HEADS + 1, dtype=F32) / DIFF_HEADS)
    n_tiles = seq // ATT_T
    n_tasks, tasks = _attn_tasks(n_tiles)
    vec = lambda p: p.astype(F32).reshape(1, -1)
    head_blk = 2 * DIFF_HEAD_DIM
    small = lambda n: pl.BlockSpec((1, n), lambda h, b: (0, 0))
    smem = pl.BlockSpec(memory_space=pltpu.SMEM)
    return pl.pallas_call(
        functools.partial(_attn_kernel, lambda_init=lambda_init, n_tasks=n_tasks, n_tiles=n_tiles),
        grid=(DIFF_HEADS, bsz),
        in_specs=[smem, smem,
                  pl.BlockSpec((None, seq, head_blk), lambda h, b: (b, 0, h)),
                  pl.BlockSpec((None, seq, head_blk), lambda h, b: (b, 0, DIFF_HEADS + h)),
                  pl.BlockSpec((None, seq, DIFF_V_DIM), lambda h, b: (b, 0, 2 * DIFF_HEADS + h)),
                  small(DIFF_HEAD_DIM), small(DIFF_HEAD_DIM), small(DIFF_HEAD_DIM), small(DIFF_HEAD_DIM),
                  small(DIFF_V_DIM)],
        out_specs=pl.BlockSpec((None, seq, DIFF_V_DIM), lambda h, b: (b, 0, h)),
        out_shape=jax.ShapeDtypeStruct((bsz, seq, DIFF_HEADS * DIFF_V_DIM), BF16),
        scratch_shapes=[pltpu.VMEM((n_tiles, DIFF_V_DIM + ATT_ONES_ROWS, ATT_T), BF16),
                        pltpu.VMEM((ATT_T, ATT_T), F32),
                        pltpu.VMEM((ATT_T, ATT_FEAT), BF16),
                        pltpu.VMEM((n_tiles, DIFF_V_DIM + ATT_FEAT, 2 * ATT_T), BF16),
                        pltpu.VMEM((ATT_T, 2 * ATT_T), F32), pltpu.VMEM((ATT_T, 2 * ATT_T), F32),
                        pltpu.VMEM((1, 2 * ATT_T), F32), pltpu.VMEM((1, 2 * ATT_T), F32),
                        pltpu.VMEM((n_tiles, 1, 2 * ATT_T), F32),
                        pltpu.VMEM((n_tiles, DIFF_V_DIM + ATT_ONES_ROWS, 2 * ATT_T), F32)],
        compiler_params=_params("arbitrary", "arbitrary"), name="diff_attention",
    )(slopes, tasks, qkv, qkv, qkv, vec(lq1), vec(lk1), vec(lq2), vec(lk2), vec(subln_w))


def kernel(x, even_w_in, even_conv_w, even_conv_b, even_dt_bias, even_a_log, even_d, even_ssd_norm_w, even_sgu_ln_g, even_sgu_ln_b, even_sgu_w, even_sgu_b, even_w_out, odd_w_qkv, odd_lambda_q1, odd_lambda_k1, odd_lambda_q2, odd_lambda_k2, odd_subln_w, odd_w_out, ln_mix_g, ln_mix_b, ln_ffn_g, ln_ffn_b, mlp_w_up, mlp_w_down):
    bsz, seq, d = x.shape
    h = x.reshape(bsz * seq, d)
    row = lambda p: p.astype(F32).reshape(1, -1)
    for l in range(DEPTH):
        i = l // 2
        if l % 2 == 0:
            w_in = even_w_in[i]
            o_z, o_xbc, o_dt, o_sgu = 0, SSD_INNER, SSD_INNER + SSD_CONV_DIM, SSD_INNER + SSD_CONV_DIM + SSD_HEADS
            w_main = jnp.concatenate([w_in[:, o_z:o_xbc], w_in[:, o_sgu:], w_in[:, o_xbc:o_dt]], axis=1).astype(BF16)
            w_dt = jnp.pad(w_in[:, o_dt:o_sgu], ((0, 0), (0, DT_PAD - SSD_HEADS))).astype(BF16)
            proj, dt_raw = _matmul(h, w_main, PROJ_TM, PROJ_TN, BF16, "in_proj", w_dt=w_dt)
            y_a = _ssd_mixer(proj, dt_raw, even_conv_w[i], even_conv_b[i], even_dt_bias[i], even_a_log[i],
                             even_d[i], even_ssd_norm_w[i], bsz, seq)
            y_b = _sgu_mixer(proj, even_sgu_ln_g[i], even_sgu_ln_b[i], even_sgu_w[i], even_sgu_b[i])
            w_out = even_w_out[i].astype(BF16)
            h = _proj_ln([y_a, y_b], [w_out[:SSD_INNER], w_out[SSD_INNER:]], h, row(ln_mix_g[l]), row(ln_mix_b[l]),
                         "even_out_proj_ln")
        else:
            lambda_init = 0.8 - 0.6 * math.exp(-0.3 * l)
            w_qkv = odd_w_qkv[i]
            q_scale = DIFF_HEAD_DIM ** -0.5 * LOG2_E
            w_qkv = jnp.concatenate([w_qkv[:, :DIFF_QK] * q_scale, w_qkv[:, DIFF_QK:]], axis=1).astype(BF16)
            qkv = _matmul(h, w_qkv, PROJ_TM, PROJ_TN, BF16, "qkv_proj").reshape(bsz, seq, -1)
            o = _diff_attention(qkv, odd_lambda_q1[i], odd_lambda_k1[i], odd_lambda_q2[i], odd_lambda_k2[i],
                                odd_subln_w[i], lambda_init, bsz, seq)
            h = _proj_ln([o.reshape(bsz * seq, -1)], [odd_w_out[i].astype(BF16)], h, row(ln_mix_g[l]), row(ln_mix_b[l]),
                         "odd_out_proj_ln")
        h = _mlp_ln(h, mlp_w_up[l].astype(BF16), mlp_w_down[l].astype(BF16), row(ln_ffn_g[l]), row(ln_ffn_b[l]),
                    "mlp_ln_%d" % l)
    return h.reshape(bsz, seq, d)
```

```python
import functools
import math

import jax
import jax.numpy as jnp
from jax import lax
from jax.experimental import pallas as pl
from jax.experimental.pallas import tpu as pltpu

F32 = jnp.float32
BF16 = jnp.bfloat16

D_MODEL = 1024
DEPTH = 2
CHUNK = 64
SSD_HEADS = 16
SSD_HEAD_DIM = 64
SSD_INNER = SSD_HEADS * SSD_HEAD_DIM
SSD_GROUPS = 2
SSD_STATE = 128
SSD_CONV = 4
SSD_BC = SSD_GROUPS * SSD_STATE
SSD_CONV_DIM = SSD_INNER + 2 * SSD_BC
SSD_GROUP_WIDTH = SSD_INNER // SSD_GROUPS
SGU_BLOCK = 128
SGU_GROUPS = 8
SGU_WIDTH = 1024
SGU_GROUP_DIM = SGU_WIDTH // SGU_GROUPS
DIFF_HEADS = 8
DIFF_HEAD_DIM = 64
DIFF_V_DIM = 2 * DIFF_HEAD_DIM
DIFF_QK = DIFF_HEADS * 2 * DIFF_HEAD_DIM
D_FF = 4 * D_MODEL
DEEPNORM_ALPHA = (2 * DEPTH) ** 0.25
LN_EPS = 1e-5

V7X_LANES = 128
V7X_SUBLANES = 8
V7X_VMEM_BYTES = 64 * 1024 * 1024
VMEM_LIMIT = 56 * 1024 * 1024

PROJ_TM = 1024
PROJ_TN = 1536
LN_TM = 512
MLP_TM = 512
MLP_FF_CHUNK = 1024
SSD_L = 256
SSD_CHUNK = 128
SGU_TM = 512
ATT_T = 512
ATT_ONES_ROWS = 16
ATT_FEAT = V7X_LANES
ATT_TASK_FIELDS = 2
ATT_UNROLL = 4
LOG2_E = 1.0 / math.log(2.0)
DT_PAD = V7X_LANES


def _params(*sem):
    return pltpu.CompilerParams(dimension_semantics=sem, vmem_limit_bytes=VMEM_LIMIT)


def _layer_norm(y, g, b):
    mu = jnp.mean(y, axis=-1, keepdims=True)
    yc = y - mu
    var = jnp.mean(yc * yc, axis=-1, keepdims=True)
    return yc * lax.rsqrt(var + LN_EPS) * g + b


def _dot(a, b):
    return jnp.dot(a, b, preferred_element_type=F32)


def _dot_nt(a, b):
    return lax.dot_general(a, b, (((1,), (1,)), ((), ())), preferred_element_type=F32)


def _matmul_kernel(x_ref, w_ref, o_ref):
    o_ref[...] = _dot(x_ref[...].astype(BF16), w_ref[...]).astype(o_ref.dtype)


def _matmul(x, w, tm, tn, out_dtype, name):
    t, k = x.shape
    n = w.shape[1]
    return pl.pallas_call(
        _matmul_kernel, grid=(t // tm, n // tn),
        in_specs=[pl.BlockSpec((tm, k), lambda i, j: (i, 0)), pl.BlockSpec((k, tn), lambda i, j: (0, j))],
        out_specs=pl.BlockSpec((tm, tn), lambda i, j: (i, j)),
        out_shape=jax.ShapeDtypeStruct((t, n), out_dtype),
        compiler_params=_params("parallel", "arbitrary"), name=name)(x, w)


def _silu(x):
    h = 0.5 * x
    return h + h * jnp.tanh(h)


def _gelu(x):
    return 0.5 * x * (1.0 + lax.erf(x * (1.0 / math.sqrt(2.0))))


IN_PROJ_SEGMENTS = ((0, SSD_INNER, _silu), (SSD_INNER, SSD_INNER + 2 * SGU_WIDTH, _gelu),
                    (SSD_INNER + 2 * SGU_WIDTH, SSD_INNER + 2 * SGU_WIDTH + SSD_CONV_DIM, None))


def _in_proj_kernel(x_ref, w_ref, wdt_ref, o_ref, dt_ref):
    xb = x_ref[...].astype(BF16)
    j = pl.program_id(1)
    for tile in range(IN_PROJ_SEGMENTS[-1][1] // PROJ_TN):
        @pl.when(j == tile)
        def _(tile=tile):
            r = _dot(xb, w_ref[...])
            lo, hi = tile * PROJ_TN, (tile + 1) * PROJ_TN
            for start, stop, act in IN_PROJ_SEGMENTS:
                a, b = max(start, lo) - lo, min(stop, hi) - lo
                if a < b:
                    part = r[:, a:b]
                    o_ref[:, a:b] = (part if act is None else act(part)).astype(o_ref.dtype)

    @pl.when(j == 0)
    def _():
        dt_ref[...] = _dot(xb, wdt_ref[...])


def _in_proj(x, w, w_dt):
    t, k = x.shape
    n = w.shape[1]
    tm, tn = PROJ_TM, PROJ_TN
    return pl.pallas_call(
        _in_proj_kernel, grid=(t // tm, n // tn),
        in_specs=[pl.BlockSpec((tm, k), lambda i, j: (i, 0)), pl.BlockSpec((k, tn), lambda i, j: (0, j)),
                  pl.BlockSpec((k, DT_PAD), lambda i, j: (0, 0))],
        out_specs=[pl.BlockSpec((tm, tn), lambda i, j: (i, j)), pl.BlockSpec((tm, DT_PAD), lambda i, j: (i, 0))],
        out_shape=[jax.ShapeDtypeStruct((t, n), BF16), jax.ShapeDtypeStruct((t, DT_PAD), F32)],
        compiler_params=_params("parallel", "arbitrary"), name="in_proj")(x, w, w_dt)


def _proj_ln_kernel(*refs, n_in):
    x_refs, w_refs = refs[:n_in], refs[n_in:2 * n_in]
    h_ref, g_ref, b_ref, o_ref = refs[2 * n_in:]
    acc = _dot(x_refs[0][...], w_refs[0][...])
    for x_ref, w_ref in zip(x_refs[1:], w_refs[1:]):
        acc += _dot(x_ref[...], w_ref[...])
    o_ref[...] = _layer_norm(DEEPNORM_ALPHA * h_ref[...] + acc, g_ref[...], b_ref[...])


def _proj_ln(xs, ws, h, g, b, name):
    t, d = h.shape
    tm = LN_TM
    n_in = len(xs)
    in_specs = ([pl.BlockSpec((tm, x.shape[1]), lambda i: (i, 0)) for x in xs]
                + [pl.BlockSpec(w.shape, lambda i: (0, 0)) for w in ws]
                + [pl.BlockSpec((tm, d), lambda i: (i, 0)),
                   pl.BlockSpec((1, d), lambda i: (0, 0)), pl.BlockSpec((1, d), lambda i: (0, 0))])
    return pl.pallas_call(
        functools.partial(_proj_ln_kernel, n_in=n_in), grid=(t // tm,), in_specs=in_specs,
        out_specs=pl.BlockSpec((tm, d), lambda i: (i, 0)),
        out_shape=jax.ShapeDtypeStruct((t, d), F32),
        compiler_params=_params("parallel"), name=name)(*xs, *ws, h, g, b)


def _mlp_ln_kernel(h_ref, wu_ref, wd_ref, g_ref, b_ref, o_ref):
    h = h_ref[...]
    hb = h.astype(BF16)
    acc = jnp.zeros(h.shape, F32)
    for c in range(D_FF // MLP_FF_CHUNK):
        cols = slice(c * MLP_FF_CHUNK, (c + 1) * MLP_FF_CHUNK)
        u = jnp.maximum(_dot(hb, wu_ref[:, cols]), 0.0)
        acc += _dot((u * u).astype(BF16), wd_ref[cols, :])
    o_ref[...] = _layer_norm(DEEPNORM_ALPHA * h + acc, g_ref[...], b_ref[...])


def _mlp_ln(h, w_up, w_down, g, b, name):
    t, d = h.shape
    tm = MLP_TM
    return pl.pallas_call(
        _mlp_ln_kernel, grid=(t // tm,),
        in_specs=[pl.BlockSpec((tm, d), lambda i: (i, 0)),
                  pl.BlockSpec(w_up.shape, lambda i: (0, 0)),
                  pl.BlockSpec(w_down.shape, lambda i: (0, 0)),
                  pl.BlockSpec((1, d), lambda i: (0, 0)), pl.BlockSpec((1, d), lambda i: (0, 0))],
        out_specs=pl.BlockSpec((tm, d), lambda i: (i, 0)),
        out_shape=jax.ShapeDtypeStruct((t, d), F32),
        compiler_params=_params("parallel"), name=name)(h, w_up, w_down, g, b)


def _softplus(x):
    return jnp.maximum(x, 0.0) + jnp.log1p(jnp.exp(-jnp.abs(x)))


def _cumsum_rows(x, segment):
    row = lax.broadcasted_iota(jnp.int32, x.shape, 0) % segment
    shift = 1
    while shift < segment:
        x = x + jnp.where(row >= shift, pltpu.roll(x, shift, 0), 0.0)
        shift *= 2
    return x


def _ssd_kernel(z_ref, xbc_ref, dtr_ref, shift_ref, cw_ref, cb_ref, dtb_ref, alog_ref, dskip_ref, nw_ref, e_ref,
                o_ref, tail_ref, state_ref):
    L, C = SSD_L, SSD_CHUNK
    sub = V7X_SUBLANES

    @pl.when(pl.program_id(1) == 0)
    def _():
        tail_ref[...] = jnp.zeros(tail_ref.shape, F32)
        state_ref[...] = jnp.zeros(state_ref.shape, F32)

    x_b = xbc_ref[...]
    x_f = x_b.astype(F32)
    conv = cb_ref[...] + cw_ref[SSD_CONV - 1:SSD_CONV, :] * x_f
    tail = tail_ref[...]
    row8 = lax.broadcasted_iota(jnp.int32, tail.shape, 0)
    head_fix = jnp.zeros(tail.shape, F32)
    for d in range(1, SSD_CONV):
        w = cw_ref[SSD_CONV - 1 - d:SSD_CONV - d, :]
        conv += w * _dot(shift_ref[d - 1], x_b)
        head_fix += w * jnp.where(row8 < d, pltpu.roll(tail, d, 0), 0.0)
    conv = jnp.concatenate([conv[:sub] + head_fix, conv[sub:]], axis=0)
    tail_ref[...] = x_f[L - sub:, :]
    xbc = _silu(conv)
    xs = xbc[:, :SSD_INNER]
    b_b = xbc[:, SSD_INNER:SSD_INNER + SSD_BC].astype(BF16)
    c_b = xbc[:, SSD_INNER + SSD_BC:].astype(BF16)

    dt = _softplus(dtr_ref[...] + dtb_ref[...])
    a_cs = _cumsum_rows(dt * (-LOG2_E * jnp.exp(alog_ref[...])), C)
    a_end = jnp.concatenate([jnp.broadcast_to(a_cs[(c + 1) * C - 1:(c + 1) * C, :], (C, DT_PAD))
                             for c in range(L // C)], axis=0)
    exp_a = jnp.exp2(a_cs)
    to_end = jnp.exp2(a_end - a_cs)
    stacked = jnp.concatenate([dt, exp_a, dt * to_end], axis=0)
    hi = stacked.astype(BF16)
    lo = (stacked - hi.astype(F32)).astype(BF16)
    expanded = _dot(jnp.concatenate([hi, lo], axis=1), e_ref[...])
    dt_e, exp_a_e, w_end_e = expanded[:L], expanded[L:2 * L], expanded[2 * L:]

    xdt_b = (xs * dt_e).astype(BF16)
    xend_b = (xs * w_end_e).astype(BF16)
    causal = (lax.broadcasted_iota(jnp.int32, (C, C), 0) >= lax.broadcasted_iota(jnp.int32, (C, C), 1))
    lane = lax.broadcasted_iota(jnp.int32, (C, V7X_LANES), 1)
    heads_per_group = SSD_HEADS // SSD_GROUPS

    y_chunks = []
    for c in range(L // C):
        rows = slice(c * C, (c + 1) * C)
        a_c = a_cs[rows]
        a_c_t = a_c.T
        y_parts = []
        for g in range(SSD_GROUPS):
            gs = slice(g * SSD_STATE, (g + 1) * SSD_STATE)
            gw = slice(g * SSD_GROUP_WIDTH, (g + 1) * SSD_GROUP_WIDTH)
            cb = _dot_nt(c_b[rows, gs], b_b[rows, gs])
            y_off = _dot(c_b[rows, gs], state_ref[g].astype(BF16)) * exp_a_e[rows, gw]
            for pair in range(heads_per_group // 2):
                h0 = g * heads_per_group + 2 * pair
                x_pair = xdt_b[rows, h0 * SSD_HEAD_DIM:(h0 + 2) * SSD_HEAD_DIM]
                zero = jnp.zeros_like(x_pair)
                mixes = []
                for h in (h0, h0 + 1):
                    seg = a_c[:, h:h + 1] - a_c_t[h:h + 1, :]
                    mixes.append((cb * jnp.exp2(jnp.where(causal, seg, -jnp.inf))).astype(BF16))
                rhs = jnp.concatenate([jnp.where(lane < SSD_HEAD_DIM, x_pair, zero),
                                       jnp.where(lane >= SSD_HEAD_DIM, x_pair, zero)], axis=0)
                y_pair = _dot(jnp.concatenate(mixes, axis=1), rhs)
                y_parts.append(y_pair + y_off[:, 2 * pair * SSD_HEAD_DIM:(2 * pair + 2) * SSD_HEAD_DIM])
            upd = lax.dot_general(b_b[rows, gs], xend_b[rows, gw], (((0,), (0,)), ((), ())),
                                  preferred_element_type=F32)
            state_ref[g] = state_ref[g] * exp_a_e[(c + 1) * C - 1:(c + 1) * C, gw] + upd
        y_chunks.append(jnp.concatenate(y_parts, axis=1))

    y = jnp.concatenate(y_chunks, axis=0) + xs * dskip_ref[...]
    y = y * z_ref[...].astype(F32)
    normed = []
    for g in range(SSD_GROUPS):
        yg = y[:, g * SSD_GROUP_WIDTH:(g + 1) * SSD_GROUP_WIDTH]
        normed.append(yg * lax.rsqrt(jnp.mean(yg * yg, axis=-1, keepdims=True) + LN_EPS))
    o_ref[...] = (jnp.concatenate(normed, axis=1) * nw_ref[...]).astype(o_ref.dtype)


def _ssd_mixer(proj, dt_raw, conv_w, conv_b, dt_bias, a_log, d_skip, norm_w, bsz, seq):
    steps = seq // SSD_L
    row = lambda b, s: b * steps + s
    pad16 = lambda v: jnp.pad(v.astype(F32), (0, DT_PAD - SSD_HEADS)).reshape(1, DT_PAD)
    head_of_channel = jnp.arange(SSD_INNER) // SSD_HEAD_DIM
    expand = (jnp.arange(DT_PAD)[:, None] == head_of_channel[None, :]).astype(BF16)
    expand = jnp.concatenate([expand, expand], axis=0)
    t_idx = jnp.arange(SSD_L)
    shifts = jnp.stack([(t_idx[:, None] - d == t_idx[None, :]) for d in range(1, SSD_CONV)]).astype(BF16)
    d_e = jnp.repeat(d_skip.astype(F32), SSD_HEAD_DIM).reshape(1, SSD_INNER)
    const = lambda shape: pl.BlockSpec(shape, lambda b, s: (0,) * len(shape))
    return pl.pallas_call(
        _ssd_kernel, grid=(bsz, steps),
        in_specs=[pl.BlockSpec((SSD_L, SSD_INNER), lambda b, s: (row(b, s), 0)),
                  pl.BlockSpec((SSD_L, SSD_CONV_DIM), lambda b, s: (row(b, s), 2)),
                  pl.BlockSpec((SSD_L, DT_PAD), lambda b, s: (row(b, s), 0)),
                  const((SSD_CONV - 1, SSD_L, SSD_L)),
                  const((SSD_CONV, SSD_CONV_DIM)), const((1, SSD_CONV_DIM)),
                  const((1, DT_PAD)), const((1, DT_PAD)),
                  const((1, SSD_INNER)), const((1, SSD_INNER)), const((2 * DT_PAD, SSD_INNER))],
        out_specs=pl.BlockSpec((SSD_L, SSD_INNER), lambda b, s: (row(b, s), 0)),
        out_shape=jax.ShapeDtypeStruct((bsz * seq, SSD_INNER), BF16),
        scratch_shapes=[pltpu.VMEM((V7X_SUBLANES, SSD_CONV_DIM), F32),
                        pltpu.VMEM((SSD_GROUPS, SSD_STATE, SSD_GROUP_WIDTH), F32)],
        compiler_params=_params("parallel", "arbitrary"), name="ssd_mixer",
    )(proj, proj, dt_raw, shifts, conv_w.astype(F32), conv_b.astype(F32).reshape(1, -1), pad16(dt_bias),
      pad16(a_log), d_e, norm_w.astype(F32).reshape(1, -1), expand)


def _sgu_kernel(u_ref, v_ref, g_ref, b_ref, w_ref, bs_ref, o_ref):
    v = _layer_norm(v_ref[...].astype(F32), g_ref[...], b_ref[...]).astype(BF16)
    r = lax.broadcasted_iota(jnp.int32, (SGU_BLOCK, SGU_BLOCK), 0) // CHUNK
    c = lax.broadcasted_iota(jnp.int32, (SGU_BLOCK, SGU_BLOCK), 1) // CHUNK
    for g in range(SGU_GROUPS):
        w = jnp.where(r >= c, w_ref[g], 0.0).astype(BF16)
        cols = slice(g * SGU_GROUP_DIM, (g + 1) * SGU_GROUP_DIM)
        for blk in range(SGU_TM // SGU_BLOCK):
            rows = slice(blk * SGU_BLOCK, (blk + 1) * SGU_BLOCK)
            mixed = _dot(w, v[rows, cols]) + bs_ref[g]
            o_ref[rows, cols] = (u_ref[rows, cols].astype(F32) * mixed).astype(o_ref.dtype)


def _sgu_mixer(proj, ln_g, ln_b, w_s, b_s):
    t = proj.shape[0]
    bias = jnp.broadcast_to(b_s.astype(F32)[:, :, None], (SGU_GROUPS, SGU_BLOCK, SGU_GROUP_DIM))
    return pl.pallas_call(
        _sgu_kernel, grid=(t // SGU_TM,),
        in_specs=[pl.BlockSpec((SGU_TM, SGU_WIDTH), lambda i: (i, 1)),
                  pl.BlockSpec((SGU_TM, SGU_WIDTH), lambda i: (i, 2)),
                  pl.BlockSpec((1, SGU_WIDTH), lambda i: (0, 0)), pl.BlockSpec((1, SGU_WIDTH), lambda i: (0, 0)),
                  pl.BlockSpec((SGU_GROUPS, SGU_BLOCK, SGU_BLOCK), lambda i: (0, 0, 0)),
                  pl.BlockSpec((SGU_GROUPS, SGU_BLOCK, SGU_GROUP_DIM), lambda i: (0, 0, 0))],
        out_specs=pl.BlockSpec((SGU_TM, SGU_WIDTH), lambda i: (i, 0)),
        out_shape=jax.ShapeDtypeStruct((t, SGU_WIDTH), BF16),
        compiler_params=_params("parallel"), name="sgu_mixer",
    )(proj, proj, ln_g.astype(F32).reshape(1, -1), ln_b.astype(F32).reshape(1, -1), w_s.astype(F32), bias)


def _attn_tasks(n_tiles):
    rows = [(i, i) for i in range(n_tiles)] + [(i, j) for i in range(n_tiles) for j in range(i)]
    return len(rows), jnp.asarray(rows, jnp.int32).reshape(-1)


def _attn_kernel(slopes_ref, tasks_ref, q_ref, k_ref, v_ref, lq1_ref, lk1_ref, lq2_ref, lk2_ref, sw_ref, o_ref,
                 vt_ref, dbias_ref, kfeat_ref, qq_ref, s0_ref, s1_ref, cmax0_ref, cmax1_ref, m_ref, acc_ref,
                 *, lambda_init, n_tasks, n_tiles):
    t = ATT_T
    s_refs, cmax_refs = (s0_ref, s1_ref), (cmax0_ref, cmax1_ref)
    head, b = pl.program_id(0), pl.program_id(1)
    slope = slopes_ref[head]
    field = lambda n, f: tasks_ref[ATT_TASK_FIELDS * n + f]

    ones = jnp.ones((ATT_ONES_ROWS, t), BF16)
    for c in range(vt_ref.shape[0]):
        vt_ref[c, :DIFF_V_DIM, :] = v_ref[c * t:(c + 1) * t, :].astype(F32).T.astype(BF16)
        vt_ref[c, DIFF_V_DIM:, :] = ones

    @pl.when(b == 0)
    def _():
        feat = lax.broadcasted_iota(jnp.int32, (ATT_FEAT, 2 * t), 0)
        for i in range(n_tiles):
            qq_ref[i, 2 * DIFF_HEAD_DIM:, :] = jnp.where(feat < 3, 1.0, 0.0).astype(F32).astype(BF16)
        key = lax.broadcasted_iota(jnp.int32, (t, t), 0)
        qry = lax.broadcasted_iota(jnp.int32, (t, t), 1)
        dbias_ref[...] = jnp.where(key // CHUNK <= qry // CHUNK,
                                   (-2.0 * slope) * jnp.maximum(key - qry, 0).astype(F32), -jnp.inf)
        bias = slope * lax.broadcasted_iota(jnp.int32, (t, ATT_FEAT), 0).astype(F32)
        hi = bias.astype(BF16).astype(F32)
        mid = (bias - hi).astype(BF16).astype(F32)
        lo = bias - hi - mid
        col = lax.broadcasted_iota(jnp.int32, (t, ATT_FEAT), 1)
        kfeat_ref[...] = jnp.where(col == 0, hi, jnp.where(col == 1, mid, jnp.where(col == 2, lo, 0.0))).astype(BF16)

    lam = (jnp.exp(jnp.sum(lq1_ref[...] * lk1_ref[...], axis=-1, keepdims=True))
           - jnp.exp(jnp.sum(lq2_ref[...] * lk2_ref[...], axis=-1, keepdims=True)) + lambda_init)

    for i in range(n_tiles):
        q_t = q_ref[i * t:(i + 1) * t, :].astype(F32).T
        feat = lax.broadcasted_iota(jnp.int32, q_t.shape, 0)
        qq_ref[i, :2 * DIFF_HEAD_DIM, :t] = jnp.where(feat < DIFF_HEAD_DIM, q_t, 0.0).astype(BF16)
        qq_ref[i, :2 * DIFF_HEAD_DIM, t:] = jnp.where(feat >= DIFF_HEAD_DIM, q_t, 0.0).astype(BF16)
    m_ref[...] = jnp.full(m_ref.shape, -jnp.inf, F32)
    acc_ref[...] = jnp.zeros(acc_ref.shape, F32)

    half = t // 2
    late_queries = (slice(half, t), slice(t + half, 2 * t))
    quadrant = lambda late_cols, fill: jnp.concatenate(
        [jnp.full((late_cols[0].shape[0], half), fill, F32), late_cols[0],
         jnp.full((late_cols[1].shape[0], half), fill, F32), late_cols[1]], axis=1)

    def scores(n, slot, diagonal):
        i, j = field(n, 0), field(n, 1)
        lhs = jnp.concatenate([k_ref[pl.ds(pl.multiple_of(j * t, t), t), :], kfeat_ref[...]], axis=1)
        if diagonal:
            dbias = dbias_ref[...]
            early = _dot(lhs[:half], qq_ref[i]) + jnp.concatenate([dbias[:half], dbias[:half]], axis=1)
            late = [_dot(lhs[half:], qq_ref[i, :, cols]) + dbias[half:, half:] for cols in late_queries]
            s = jnp.concatenate([early, quadrant(late, -jnp.inf)], axis=0)
        else:
            s = _dot(lhs, qq_ref[i])
        s_refs[slot][...] = s
        cmax_refs[slot][...] = jnp.max(s, axis=0, keepdims=True)

    def accumulate(n, slot, diagonal):
        i, j = field(n, 0), field(n, 1)
        shift = slope * ((j - i) * t).astype(F32)
        m = m_ref[i]
        m_new = jnp.maximum(m, cmax_refs[slot][...] + shift)
        alpha = jnp.exp2(m - m_new)
        p = jnp.exp2(s_refs[slot][...] - (m_new - shift)).astype(BF16)
        m_ref[i] = m_new
        if diagonal:
            late = [_dot(vt_ref[j, :, half:], p[half:, cols]) for cols in late_queries]
            pv = _dot(vt_ref[j, :, :half], p[:half]) + quadrant(late, 0.0)
        else:
            pv = _dot(vt_ref[j], p)
        acc_ref[i] = alpha * acc_ref[i] + pv

    def stage(n, scores_diagonal, consume_diagonal):
        scores(n + 1, 1 - n % 2, scores_diagonal)
        accumulate(n, n % 2, consume_diagonal)

    def run_stages(first, stop, diagonal):
        if first < stop and first % 2 == 1:
            stage(first, diagonal, diagonal)
            first += 1
        rounds = (stop - first) // ATT_UNROLL

        def body(r, carry):
            n = first + ATT_UNROLL * r
            for u in range(ATT_UNROLL):
                scores(n + u + 1, (u + 1) % 2, diagonal)
                accumulate(n + u, u % 2, diagonal)
            return carry

        if rounds > 0:
            lax.fori_loop(0, rounds, body, 0)
        for n in range(first + ATT_UNROLL * rounds, stop):
            stage(n, diagonal, diagonal)

    scores(0, 0, diagonal=True)
    run_stages(0, n_tiles - 1, diagonal=True)
    if n_tasks > n_tiles:
        stage(n_tiles - 1, False, True)
        run_stages(n_tiles, n_tasks - 1, diagonal=False)
    accumulate(n_tasks - 1, (n_tasks - 1) % 2, diagonal=n_tasks == n_tiles)

    for i in range(n_tiles):
        o = acc_ref[i, :DIFF_V_DIM, :] * (1.0 / acc_ref[i, DIFF_V_DIM:DIFF_V_DIM + 1, :])
        o = o[:, :t] - lam * o[:, t:]
        o = o * lax.rsqrt(jnp.mean(o * o, axis=0, keepdims=True) + LN_EPS)
        o_ref[i * t:(i + 1) * t, :] = (o.T * sw_ref[...] * (1.0 - lambda_init)).astype(o_ref.dtype)


def _diff_attention(qkv, lq1, lk1, lq2, lk2, subln_w, lambda_init, bsz, seq):
    slopes = LOG2_E * jnp.exp2(-8.0 * jnp.arange(1, DIFF_HEADS + 1, dtype=F32) / DIFF_HEADS)
    n_tiles = seq // ATT_T
    n_tasks, tasks = _attn_tasks(n_tiles)
    vec = lambda p: p.astype(F32).reshape(1, -1)
    head_blk = 2 * DIFF_HEAD_DIM
    small = lambda n: pl.BlockSpec((1, n), lambda h, b: (0, 0))
    smem = pl.BlockSpec(memory_space=pltpu.SMEM)
    return pl.pallas_call(
        functools.partial(_attn_kernel, lambda_init=lambda_init, n_tasks=n_tasks, n_tiles=n_tiles),
        grid=(DIFF_HEADS, bsz),
        in_specs=[smem, smem,
                  pl.BlockSpec((None, seq, head_blk), lambda h, b: (b, 0, h)),
                  pl.BlockSpec((None, seq, head_blk), lambda h, b: (b, 0, DIFF_HEADS + h)),
                  pl.BlockSpec((None, seq, DIFF_V_DIM), lambda h, b: (b, 0, 2 * DIFF_HEADS + h)),
                  small(DIFF_HEAD_DIM), small(DIFF_HEAD_DIM), small(DIFF_HEAD_DIM), small(DIFF_HEAD_DIM),
                  small(DIFF_V_DIM)],
        out_specs=pl.BlockSpec((None, seq, DIFF_V_DIM), lambda h, b: (b, 0, h)),
        out_shape=jax.ShapeDtypeStruct((bsz, seq, DIFF_HEADS * DIFF_V_DIM), BF16),
        scratch_shapes=[pltpu.VMEM((n_tiles, DIFF_V_DIM + ATT_ONES_ROWS, ATT_T), BF16),
                        pltpu.VMEM((ATT_T, ATT_T), F32),
                        pltpu.VMEM((ATT_T, ATT_FEAT), BF16),
                        pltpu.VMEM((n_tiles, DIFF_V_DIM + ATT_FEAT, 2 * ATT_T), BF16),
                        pltpu.VMEM((ATT_T, 2 * ATT_T), F32), pltpu.VMEM((ATT_T, 2 * ATT_T), F32),
                        pltpu.VMEM((1, 2 * ATT_T), F32), pltpu.VMEM((1, 2 * ATT_T), F32),
                        pltpu.VMEM((n_tiles, 1, 2 * ATT_T), F32),
                        pltpu.VMEM((n_tiles, DIFF_V_DIM + ATT_ONES_ROWS, 2 * ATT_T), F32)],
        compiler_params=_params("arbitrary", "arbitrary"), name="diff_attention",
    )(slopes, tasks, qkv, qkv, qkv, vec(lq1), vec(lk1), vec(lq2), vec(lk2), vec(subln_w))


def kernel(x, even_w_in, even_conv_w, even_conv_b, even_dt_bias, even_a_log, even_d, even_ssd_norm_w, even_sgu_ln_g, even_sgu_ln_b, even_sgu_w, even_sgu_b, even_w_out, odd_w_qkv, odd_lambda_q1, odd_lambda_k1, odd_lambda_q2, odd_lambda_k2, odd_subln_w, odd_w_out, ln_mix_g, ln_mix_b, ln_ffn_g, ln_ffn_b, mlp_w_up, mlp_w_down):
    bsz, seq, d = x.shape
    h = x.reshape(bsz * seq, d)
    row = lambda p: p.astype(F32).reshape(1, -1)
    for l in range(DEPTH):
        i = l // 2
        if l % 2 == 0:
            w_in = even_w_in[i]
            o_z, o_xbc, o_dt, o_sgu = 0, SSD_INNER, SSD_INNER + SSD_CONV_DIM, SSD_INNER + SSD_CONV_DIM + SSD_HEADS
            w_main = jnp.concatenate([w_in[:, o_z:o_xbc], w_in[:, o_sgu:], w_in[:, o_xbc:o_dt]], axis=1).astype(BF16)
            w_dt = jnp.pad(w_in[:, o_dt:o_sgu], ((0, 0), (0, DT_PAD - SSD_HEADS))).astype(BF16)
            proj, dt_raw = _in_proj(h, w_main, w_dt)
            y_a = _ssd_mixer(proj, dt_raw, even_conv_w[i], even_conv_b[i], even_dt_bias[i], even_a_log[i],
                             even_d[i], even_ssd_norm_w[i], bsz, seq)
            y_b = _sgu_mixer(proj, even_sgu_ln_g[i], even_sgu_ln_b[i], even_sgu_w[i], even_sgu_b[i])
            w_out = even_w_out[i].astype(BF16)
            h = _proj_ln([y_a, y_b], [w_out[:SSD_INNER], w_out[SSD_INNER:]], h, row(ln_mix_g[l]), row(ln_mix_b[l]),
                         "even_out_proj_ln")
        else:
            lambda_init = 0.8 - 0.6 * math.exp(-0.3 * l)
            w_qkv = odd_w_qkv[i]
            q_scale = DIFF_HEAD_DIM ** -0.5 * LOG2_E
            w_qkv = jnp.concatenate([w_qkv[:, :DIFF_QK] * q_scale, w_qkv[:, DIFF_QK:]], axis=1).astype(BF16)
            qkv = _matmul(h, w_qkv, PROJ_TM, PROJ_TN, BF16, "qkv_proj").reshape(bsz, seq, -1)
            o = _diff_attention(qkv, odd_lambda_q1[i], odd_lambda_k1[i], odd_lambda_q2[i], odd_lambda_k2[i],
                                odd_subln_w[i], lambda_init, bsz, seq)
            h = _proj_ln([o.reshape(bsz * seq, -1)], [odd_w_out[i].astype(BF16)], h, row(ln_mix_g[l]), row(ln_mix_b[l]),
                         "odd_out_proj_ln")
        h = _mlp_ln(h, mlp_w_up[l].astype(BF16), mlp_w_down[l].astype(BF16), row(ln_ffn_g[l]), row(ln_ffn_b[l]),
                    "mlp_ln_%d" % l)
    return h.reshape(bsz, seq, d)
```

```python
import functools
import math

import jax
import jax.numpy as jnp
from jax import lax
from jax.experimental import pallas as pl
from jax.experimental.pallas import tpu as pltpu

F32 = jnp.float32
BF16 = jnp.bfloat16

D_MODEL = 1024
DEPTH = 2
CHUNK = 64
SSD_HEADS = 16
SSD_HEAD_DIM = 64
SSD_INNER = SSD_HEADS * SSD_HEAD_DIM
SSD_GROUPS = 2
SSD_STATE = 128
SSD_CONV = 4
SSD_BC = SSD_GROUPS * SSD_STATE
SSD_CONV_DIM = SSD_INNER + 2 * SSD_BC
SSD_GROUP_WIDTH = SSD_INNER // SSD_GROUPS
SGU_BLOCK = 128
SGU_GROUPS = 8
SGU_WIDTH = 1024
SGU_GROUP_DIM = SGU_WIDTH // SGU_GROUPS
DIFF_HEADS = 8
DIFF_HEAD_DIM = 64
DIFF_V_DIM = 2 * DIFF_HEAD_DIM
DIFF_QK = DIFF_HEADS * 2 * DIFF_HEAD_DIM
D_FF = 4 * D_MODEL
DEEPNORM_ALPHA = (2 * DEPTH) ** 0.25
LN_EPS = 1e-5

V7X_LANES = 128
V7X_SUBLANES = 8
V7X_VMEM_BYTES = 64 * 1024 * 1024
VMEM_LIMIT = 56 * 1024 * 1024

PROJ_TM = 1024
IN_PROJ_TM = 512
PROJ_TN = 1536
LN_TM = 512
MLP_TM = 512
MLP_FF_CHUNK = 1024
SSD_L = 256
SSD_CHUNK = 128
SGU_TM = 512
ATT_T = 512
ATT_ONES_ROWS = 16
ATT_FEAT = V7X_LANES
ATT_TASK_FIELDS = 2
ATT_UNROLL = 4
LOG2_E = 1.0 / math.log(2.0)
DT_PAD = V7X_LANES


def _params(*sem):
    return pltpu.CompilerParams(dimension_semantics=sem, vmem_limit_bytes=VMEM_LIMIT)


def _layer_norm(y, g, b):
    mu = jnp.mean(y, axis=-1, keepdims=True)
    yc = y - mu
    var = jnp.mean(yc * yc, axis=-1, keepdims=True)
    return yc * lax.rsqrt(var + LN_EPS) * g + b


def _dot(a, b):
    return jnp.dot(a, b, preferred_element_type=F32)


def _dot_nt(a, b):
    return lax.dot_general(a, b, (((1,), (1,)), ((), ())), preferred_element_type=F32)


def _project(x_ref, w_ref, o_ref):
    xb = x_ref[...].astype(BF16)
    for c in range(w_ref.shape[1] // PROJ_TN):
        cols = slice(c * PROJ_TN, (c + 1) * PROJ_TN)
        o_ref[:, cols] = _dot(xb, w_ref[:, cols]).astype(o_ref.dtype)
    return xb


def _matmul_kernel(x_ref, w_ref, o_ref):
    _project(x_ref, w_ref, o_ref)


def _matmul(x, w, tm, out_dtype, name):
    t, k = x.shape
    n = w.shape[1]
    return pl.pallas_call(
        _matmul_kernel, grid=(t // tm,),
        in_specs=[pl.BlockSpec((tm, k), lambda i: (i, 0)), pl.BlockSpec((k, n), lambda i: (0, 0))],
        out_specs=pl.BlockSpec((tm, n), lambda i: (i, 0)),
        out_shape=jax.ShapeDtypeStruct((t, n), out_dtype),
        compiler_params=_params("parallel"), name=name)(x, w)


def _silu(x):
    h = 0.5 * x
    return h + h * jnp.tanh(h)


def _gelu(x):
    return 0.5 * x * (1.0 + lax.erf(x * (1.0 / math.sqrt(2.0))))


def _in_proj_kernel(x_ref, w_ref, wdt_ref, o_ref, dt_ref):
    xb = _project(x_ref, w_ref, o_ref)
    dt_ref[...] = _dot(xb, wdt_ref[...])


def _in_proj(x, w, w_dt):
    t, k = x.shape
    n = w.shape[1]
    tm = IN_PROJ_TM
    return pl.pallas_call(
        _in_proj_kernel, grid=(t // tm,),
        in_specs=[pl.BlockSpec((tm, k), lambda i: (i, 0)), pl.BlockSpec((k, n), lambda i: (0, 0)),
                  pl.BlockSpec((k, DT_PAD), lambda i: (0, 0))],
        out_specs=[pl.BlockSpec((tm, n), lambda i: (i, 0)), pl.BlockSpec((tm, DT_PAD), lambda i: (i, 0))],
        out_shape=[jax.ShapeDtypeStruct((t, n), BF16), jax.ShapeDtypeStruct((t, DT_PAD), F32)],
        compiler_params=_params("parallel"), name="in_proj")(x, w, w_dt)


def _proj_ln_kernel(*refs, n_in):
    x_refs, w_refs = refs[:n_in], refs[n_in:2 * n_in]
    h_ref, g_ref, b_ref, o_ref = refs[2 * n_in:]
    acc = _dot(x_refs[0][...], w_refs[0][...])
    for x_ref, w_ref in zip(x_refs[1:], w_refs[1:]):
        acc += _dot(x_ref[...], w_ref[...])
    o_ref[...] = _layer_norm(DEEPNORM_ALPHA * h_ref[...] + acc, g_ref[...], b_ref[...])


def _proj_ln(xs, ws, h, g, b, name):
    t, d = h.shape
    tm = LN_TM
    n_in = len(xs)
    in_specs = ([pl.BlockSpec((tm, x.shape[1]), lambda i: (i, 0)) for x in xs]
                + [pl.BlockSpec(w.shape, lambda i: (0, 0)) for w in ws]
                + [pl.BlockSpec((tm, d), lambda i: (i, 0)),
                   pl.BlockSpec((1, d), lambda i: (0, 0)), pl.BlockSpec((1, d), lambda i: (0, 0))])
    return pl.pallas_call(
        functools.partial(_proj_ln_kernel, n_in=n_in), grid=(t // tm,), in_specs=in_specs,
        out_specs=pl.BlockSpec((tm, d), lambda i: (i, 0)),
        out_shape=jax.ShapeDtypeStruct((t, d), F32),
        compiler_params=_params("parallel"), name=name)(*xs, *ws, h, g, b)


def _mlp_ln_kernel(h_ref, wu_ref, wd_ref, g_ref, b_ref, o_ref):
    h = h_ref[...]
    hb = h.astype(BF16)
    acc = jnp.zeros(h.shape, F32)
    for c in range(D_FF // MLP_FF_CHUNK):
        cols = slice(c * MLP_FF_CHUNK, (c + 1) * MLP_FF_CHUNK)
        u = jnp.maximum(_dot(hb, wu_ref[:, cols]), 0.0)
        acc += _dot((u * u).astype(BF16), wd_ref[cols, :])
    o_ref[...] = _layer_norm(DEEPNORM_ALPHA * h + acc, g_ref[...], b_ref[...])


def _mlp_ln(h, w_up, w_down, g, b, name):
    t, d = h.shape
    tm = MLP_TM
    return pl.pallas_call(
        _mlp_ln_kernel, grid=(t // tm,),
        in_specs=[pl.BlockSpec((tm, d), lambda i: (i, 0)),
                  pl.BlockSpec(w_up.shape, lambda i: (0, 0)),
                  pl.BlockSpec(w_down.shape, lambda i: (0, 0)),
                  pl.BlockSpec((1, d), lambda i: (0, 0)), pl.BlockSpec((1, d), lambda i: (0, 0))],
        out_specs=pl.BlockSpec((tm, d), lambda i: (i, 0)),
        out_shape=jax.ShapeDtypeStruct((t, d), F32),
        compiler_params=_params("parallel"), name=name)(h, w_up, w_down, g, b)


def _softplus(x):
    return jnp.maximum(x, 0.0) + jnp.log1p(jnp.exp(-jnp.abs(x)))


def _cumsum_rows(x, segment):
    row = lax.broadcasted_iota(jnp.int32, x.shape, 0) % segment
    shift = 1
    while shift < segment:
        x = x + jnp.where(row >= shift, pltpu.roll(x, shift, 0), 0.0)
        shift *= 2
    return x


def _ssd_kernel(z_ref, xbc_ref, dtr_ref, shift_ref, cw_ref, cb_ref, dtb_ref, alog_ref, dskip_ref, nw_ref, e_ref,
                o_ref, tail_ref, state_ref):
    L, C = SSD_L, SSD_CHUNK
    sub = V7X_SUBLANES

    @pl.when(pl.program_id(1) == 0)
    def _():
        tail_ref[...] = jnp.zeros(tail_ref.shape, F32)
        state_ref[...] = jnp.zeros(state_ref.shape, F32)

    x_b = xbc_ref[...]
    x_f = x_b.astype(F32)
    conv = cb_ref[...] + cw_ref[SSD_CONV - 1:SSD_CONV, :] * x_f
    tail = tail_ref[...]
    row8 = lax.broadcasted_iota(jnp.int32, tail.shape, 0)
    head_fix = jnp.zeros(tail.shape, F32)
    for d in range(1, SSD_CONV):
        w = cw_ref[SSD_CONV - 1 - d:SSD_CONV - d, :]
        conv += w * _dot(shift_ref[d - 1], x_b)
        head_fix += w * jnp.where(row8 < d, pltpu.roll(tail, d, 0), 0.0)
    conv = jnp.concatenate([conv[:sub] + head_fix, conv[sub:]], axis=0)
    tail_ref[...] = x_f[L - sub:, :]
    xbc = _silu(conv)
    xs = xbc[:, :SSD_INNER]
    b_b = xbc[:, SSD_INNER:SSD_INNER + SSD_BC].astype(BF16)
    c_b = xbc[:, SSD_INNER + SSD_BC:].astype(BF16)

    dt = _softplus(dtr_ref[...] + dtb_ref[...])
    a_cs = _cumsum_rows(dt * (-LOG2_E * jnp.exp(alog_ref[...])), C)
    a_end = jnp.concatenate([jnp.broadcast_to(a_cs[(c + 1) * C - 1:(c + 1) * C, :], (C, DT_PAD))
                             for c in range(L // C)], axis=0)
    exp_a = jnp.exp2(a_cs)
    to_end = jnp.exp2(a_end - a_cs)
    stacked = jnp.concatenate([dt, exp_a, dt * to_end], axis=0)
    hi = stacked.astype(BF16)
    lo = (stacked - hi.astype(F32)).astype(BF16)
    expanded = _dot(jnp.concatenate([hi, lo], axis=1), e_ref[...])
    dt_e, exp_a_e, w_end_e = expanded[:L], expanded[L:2 * L], expanded[2 * L:]

    xdt_b = (xs * dt_e).astype(BF16)
    xend_b = (xs * w_end_e).astype(BF16)
    causal = (lax.broadcasted_iota(jnp.int32, (C, C), 0) >= lax.broadcasted_iota(jnp.int32, (C, C), 1))
    lane = lax.broadcasted_iota(jnp.int32, (C, V7X_LANES), 1)
    heads_per_group = SSD_HEADS // SSD_GROUPS

    y_chunks = []
    for c in range(L // C):
        rows = slice(c * C, (c + 1) * C)
        a_c = a_cs[rows]
        a_c_t = a_c.T
        y_parts = []
        for g in range(SSD_GROUPS):
            gs = slice(g * SSD_STATE, (g + 1) * SSD_STATE)
            gw = slice(g * SSD_GROUP_WIDTH, (g + 1) * SSD_GROUP_WIDTH)
            cb = _dot_nt(c_b[rows, gs], b_b[rows, gs])
            y_off = _dot(c_b[rows, gs], state_ref[g].astype(BF16)) * exp_a_e[rows, gw]
            for pair in range(heads_per_group // 2):
                h0 = g * heads_per_group + 2 * pair
                x_pair = xdt_b[rows, h0 * SSD_HEAD_DIM:(h0 + 2) * SSD_HEAD_DIM]
                zero = jnp.zeros_like(x_pair)
                mixes = []
                for h in (h0, h0 + 1):
                    seg = a_c[:, h:h + 1] - a_c_t[h:h + 1, :]
                    mixes.append((cb * jnp.exp2(jnp.where(causal, seg, -jnp.inf))).astype(BF16))
                rhs = jnp.concatenate([jnp.where(lane < SSD_HEAD_DIM, x_pair, zero),
                                       jnp.where(lane >= SSD_HEAD_DIM, x_pair, zero)], axis=0)
                y_pair = _dot(jnp.concatenate(mixes, axis=1), rhs)
                y_parts.append(y_pair + y_off[:, 2 * pair * SSD_HEAD_DIM:(2 * pair + 2) * SSD_HEAD_DIM])
            upd = lax.dot_general(b_b[rows, gs], xend_b[rows, gw], (((0,), (0,)), ((), ())),
                                  preferred_element_type=F32)
            state_ref[g] = state_ref[g] * exp_a_e[(c + 1) * C - 1:(c + 1) * C, gw] + upd
        y_chunks.append(jnp.concatenate(y_parts, axis=1))

    y = jnp.concatenate(y_chunks, axis=0) + xs * dskip_ref[...]
    y = y * _silu(z_ref[...].astype(F32))
    normed = []
    for g in range(SSD_GROUPS):
        yg = y[:, g * SSD_GROUP_WIDTH:(g + 1) * SSD_GROUP_WIDTH]
        normed.append(yg * lax.rsqrt(jnp.mean(yg * yg, axis=-1, keepdims=True) + LN_EPS))
    o_ref[...] = (jnp.concatenate(normed, axis=1) * nw_ref[...]).astype(o_ref.dtype)


def _ssd_mixer(proj, dt_raw, conv_w, conv_b, dt_bias, a_log, d_skip, norm_w, bsz, seq):
    steps = seq // SSD_L
    row = lambda b, s: b * steps + s
    pad16 = lambda v: jnp.pad(v.astype(F32), (0, DT_PAD - SSD_HEADS)).reshape(1, DT_PAD)
    head_of_channel = jnp.arange(SSD_INNER) // SSD_HEAD_DIM
    expand = (jnp.arange(DT_PAD)[:, None] == head_of_channel[None, :]).astype(BF16)
    expand = jnp.concatenate([expand, expand], axis=0)
    t_idx = jnp.arange(SSD_L)
    shifts = jnp.stack([(t_idx[:, None] - d == t_idx[None, :]) for d in range(1, SSD_CONV)]).astype(BF16)
    d_e = jnp.repeat(d_skip.astype(F32), SSD_HEAD_DIM).reshape(1, SSD_INNER)
    const = lambda shape: pl.BlockSpec(shape, lambda b, s: (0,) * len(shape))
    return pl.pallas_call(
        _ssd_kernel, grid=(bsz, steps),
        in_specs=[pl.BlockSpec((SSD_L, SSD_INNER), lambda b, s: (row(b, s), 0)),
                  pl.BlockSpec((SSD_L, SSD_CONV_DIM), lambda b, s: (row(b, s), 2)),
                  pl.BlockSpec((SSD_L, DT_PAD), lambda b, s: (row(b, s), 0)),
                  const((SSD_CONV - 1, SSD_L, SSD_L)),
                  const((SSD_CONV, SSD_CONV_DIM)), const((1, SSD_CONV_DIM)),
                  const((1, DT_PAD)), const((1, DT_PAD)),
                  const((1, SSD_INNER)), const((1, SSD_INNER)), const((2 * DT_PAD, SSD_INNER))],
        out_specs=pl.BlockSpec((SSD_L, SSD_INNER), lambda b, s: (row(b, s), 0)),
        out_shape=jax.ShapeDtypeStruct((bsz * seq, SSD_INNER), BF16),
        scratch_shapes=[pltpu.VMEM((V7X_SUBLANES, SSD_CONV_DIM), F32),
                        pltpu.VMEM((SSD_GROUPS, SSD_STATE, SSD_GROUP_WIDTH), F32)],
        compiler_params=_params("parallel", "arbitrary"), name="ssd_mixer",
    )(proj, proj, dt_raw, shifts, conv_w.astype(F32), conv_b.astype(F32).reshape(1, -1), pad16(dt_bias),
      pad16(a_log), d_e, norm_w.astype(F32).reshape(1, -1), expand)


def _sgu_kernel(u_ref, v_ref, g_ref, b_ref, w_ref, bs_ref, o_ref):
    v = _layer_norm(_gelu(v_ref[...].astype(F32)), g_ref[...], b_ref[...]).astype(BF16)
    r = lax.broadcasted_iota(jnp.int32, (SGU_BLOCK, SGU_BLOCK), 0) // CHUNK
    c = lax.broadcasted_iota(jnp.int32, (SGU_BLOCK, SGU_BLOCK), 1) // CHUNK
    for g in range(SGU_GROUPS):
        w = jnp.where(r >= c, w_ref[g], 0.0).astype(BF16)
        cols = slice(g * SGU_GROUP_DIM, (g + 1) * SGU_GROUP_DIM)
        for blk in range(SGU_TM // SGU_BLOCK):
            rows = slice(blk * SGU_BLOCK, (blk + 1) * SGU_BLOCK)
            mixed = _dot(w, v[rows, cols]) + bs_ref[g]
            o_ref[rows, cols] = (_gelu(u_ref[rows, cols].astype(F32)) * mixed).astype(o_ref.dtype)


def _sgu_mixer(proj, ln_g, ln_b, w_s, b_s):
    t = proj.shape[0]
    bias = jnp.broadcast_to(b_s.astype(F32)[:, :, None], (SGU_GROUPS, SGU_BLOCK, SGU_GROUP_DIM))
    return pl.pallas_call(
        _sgu_kernel, grid=(t // SGU_TM,),
        in_specs=[pl.BlockSpec((SGU_TM, SGU_WIDTH), lambda i: (i, 1)),
                  pl.BlockSpec((SGU_TM, SGU_WIDTH), lambda i: (i, 2)),
                  pl.BlockSpec((1, SGU_WIDTH), lambda i: (0, 0)), pl.BlockSpec((1, SGU_WIDTH), lambda i: (0, 0)),
                  pl.BlockSpec((SGU_GROUPS, SGU_BLOCK, SGU_BLOCK), lambda i: (0, 0, 0)),
                  pl.BlockSpec((SGU_GROUPS, SGU_BLOCK, SGU_GROUP_DIM), lambda i: (0, 0, 0))],
        out_specs=pl.BlockSpec((SGU_TM, SGU_WIDTH), lambda i: (i, 0)),
        out_shape=jax.ShapeDtypeStruct((t, SGU_WIDTH), BF16),
        compiler_params=_params("parallel"), name="sgu_mixer",
    )(proj, proj, ln_g.astype(F32).reshape(1, -1), ln_b.astype(F32).reshape(1, -1), w_s.astype(F32), bias)


def _attn_tasks(n_tiles):
    rows = [(i, i) for i in range(n_tiles)] + [(i, j) for i in range(n_tiles) for j in range(i)]
    return len(rows), jnp.asarray(rows, jnp.int32).reshape(-1)


def _attn_kernel(slopes_ref, tasks_ref, q_ref, k_ref, v_ref, lq1_ref, lk1_ref, lq2_ref, lk2_ref, sw_ref, o_ref,
                 vt_ref, dbias_ref, kfeat_ref, qq_ref, s0_ref, s1_ref, cmax0_ref, cmax1_ref, m_ref, acc_ref,
                 *, lambda_init, n_tasks, n_tiles):
    t = ATT_T
    s_refs, cmax_refs = (s0_ref, s1_ref), (cmax0_ref, cmax1_ref)
    head, b = pl.program_id(0), pl.program_id(1)
    slope = slopes_ref[head]
    field = lambda n, f: tasks_ref[ATT_TASK_FIELDS * n + f]

    @pl.when(b == 0)
    def _():
        feat = lax.broadcasted_iota(jnp.int32, (ATT_FEAT, 2 * t), 0)
        for i in range(n_tiles):
            qq_ref[i, 2 * DIFF_HEAD_DIM:, :] = jnp.where(feat < 3, 1.0, 0.0).astype(F32).astype(BF16)
        key = lax.broadcasted_iota(jnp.int32, (t, t), 0)
        qry = lax.broadcasted_iota(jnp.int32, (t, t), 1)
        dbias_ref[...] = jnp.where(key // CHUNK <= qry // CHUNK,
                                   (-2.0 * slope) * jnp.maximum(key - qry, 0).astype(F32), -jnp.inf)
        bias = slope * lax.broadcasted_iota(jnp.int32, (t, ATT_FEAT), 0).astype(F32)
        hi = bias.astype(BF16).astype(F32)
        mid = (bias - hi).astype(BF16).astype(F32)
        lo = bias - hi - mid
        col = lax.broadcasted_iota(jnp.int32, (t, ATT_FEAT), 1)
        kfeat_ref[...] = jnp.where(col == 0, hi, jnp.where(col == 1, mid, jnp.where(col == 2, lo, 0.0))).astype(BF16)

    lam = (jnp.exp(jnp.sum(lq1_ref[...] * lk1_ref[...], axis=-1, keepdims=True))
           - jnp.exp(jnp.sum(lq2_ref[...] * lk2_ref[...], axis=-1, keepdims=True)) + lambda_init)

    ones = jnp.ones((ATT_ONES_ROWS, t), BF16)
    for c in range(n_tiles):
        vt_ref[c, :DIFF_V_DIM, :] = v_ref[c * t:(c + 1) * t, :].astype(F32).T.astype(BF16)
        vt_ref[c, DIFF_V_DIM:, :] = ones
    for i in range(n_tiles):
        q_t = q_ref[i * t:(i + 1) * t, :].astype(F32).T
        feat = lax.broadcasted_iota(jnp.int32, q_t.shape, 0)
        qq_ref[i, :2 * DIFF_HEAD_DIM, :t] = jnp.where(feat < DIFF_HEAD_DIM, q_t, 0.0).astype(BF16)
        qq_ref[i, :2 * DIFF_HEAD_DIM, t:] = jnp.where(feat >= DIFF_HEAD_DIM, q_t, 0.0).astype(BF16)

    half = t // 2
    late_queries = (slice(half, t), slice(t + half, 2 * t))
    quadrant = lambda late_cols, fill: jnp.concatenate(
        [jnp.full((late_cols[0].shape[0], half), fill, F32), late_cols[0],
         jnp.full((late_cols[1].shape[0], half), fill, F32), late_cols[1]], axis=1)

    def scores(n, slot, diagonal):
        i, j = field(n, 0), field(n, 1)
        lhs = jnp.concatenate([k_ref[pl.ds(pl.multiple_of(j * t, t), t), :], kfeat_ref[...]], axis=1)
        if diagonal:
            dbias = dbias_ref[...]
            early = _dot(lhs[:half], qq_ref[i]) + jnp.concatenate([dbias[:half], dbias[:half]], axis=1)
            late = [_dot(lhs[half:], qq_ref[i, :, cols]) + dbias[half:, half:] for cols in late_queries]
            s = jnp.concatenate([early, quadrant(late, -jnp.inf)], axis=0)
        else:
            s = _dot(lhs, qq_ref[i])
        s_refs[slot][...] = s
        cmax_refs[slot][...] = jnp.max(s, axis=0, keepdims=True)

    def accumulate(n, slot, diagonal):
        i, j = field(n, 0), field(n, 1)
        if diagonal:
            m_new = cmax_refs[slot][...]
            p = jnp.exp2(s_refs[slot][...] - m_new).astype(BF16)
            late = [_dot(vt_ref[j, :, half:], p[half:, cols]) for cols in late_queries]
            acc_ref[i] = _dot(vt_ref[j, :, :half], p[:half]) + quadrant(late, 0.0)
        else:
            shift = slope * ((j - i) * t).astype(F32)
            m = m_ref[i]
            m_new = jnp.maximum(m, cmax_refs[slot][...] + shift)
            alpha = jnp.exp2(m - m_new)
            p = jnp.exp2(s_refs[slot][...] - (m_new - shift)).astype(BF16)
            acc_ref[i] = alpha * acc_ref[i] + _dot(vt_ref[j], p)
        m_ref[i] = m_new

    def stage(n, scores_diagonal, consume_diagonal):
        scores(n + 1, 1 - n % 2, scores_diagonal)
        accumulate(n, n % 2, consume_diagonal)

    def finish(i):
        o = acc_ref[i, :DIFF_V_DIM, :] * (1.0 / acc_ref[i, DIFF_V_DIM:DIFF_V_DIM + 1, :])
        o = o[:, :t] - lam * o[:, t:]
        o = o * lax.rsqrt(jnp.mean(o * o, axis=0, keepdims=True) + LN_EPS)
        o_ref[i * t:(i + 1) * t, :] = (o.T * sw_ref[...] * (1.0 - lambda_init)).astype(o_ref.dtype)

    def run_stages(first, stop, diagonal):
        if first < stop and first % 2 == 1:
            stage(first, diagonal, diagonal)
            first += 1
        rounds = (stop - first) // ATT_UNROLL

        def body(r, carry):
            n = first + ATT_UNROLL * r
            for u in range(ATT_UNROLL):
                scores(n + u + 1, (u + 1) % 2, diagonal)
                accumulate(n + u, u % 2, diagonal)
            return carry

        if rounds > 0:
            lax.fori_loop(0, rounds, body, 0)
        for n in range(first + ATT_UNROLL * rounds, stop):
            stage(n, diagonal, diagonal)

    scores(0, 0, diagonal=True)
    run_stages(0, n_tiles - 1, diagonal=True)
    if n_tasks > n_tiles:
        stage(n_tiles - 1, False, True)
        run_stages(n_tiles, n_tasks - 1, diagonal=False)
    accumulate(n_tasks - 1, (n_tasks - 1) % 2, diagonal=n_tasks == n_tiles)
    for i in range(n_tiles):
        finish(i)


def _diff_attention(qkv, lq1, lk1, lq2, lk2, subln_w, lambda_init, bsz, seq):
    slopes = LOG2_E * jnp.exp2(-8.0 * jnp.arange(1, DIFF_HEADS + 1, dtype=F32) / DIFF_HEADS)
    n_tiles = seq // ATT_T
    n_tasks, tasks = _attn_tasks(n_tiles)
    vec = lambda p: p.astype(F32).reshape(1, -1)
    head_blk = 2 * DIFF_HEAD_DIM
    small = lambda n: pl.BlockSpec((1, n), lambda h, b: (0, 0))
    smem = pl.BlockSpec(memory_space=pltpu.SMEM)
    return pl.pallas_call(
        functools.partial(_attn_kernel, lambda_init=lambda_init, n_tasks=n_tasks, n_tiles=n_tiles),
        grid=(DIFF_HEADS, bsz),
        in_specs=[smem, smem,
                  pl.BlockSpec((None, seq, head_blk), lambda h, b: (b, 0, h)),
                  pl.BlockSpec((None, seq, head_blk), lambda h, b: (b, 0, DIFF_HEADS + h)),
                  pl.BlockSpec((None, seq, DIFF_V_DIM), lambda h, b: (b, 0, 2 * DIFF_HEADS + h)),
                  small(DIFF_HEAD_DIM), small(DIFF_HEAD_DIM), small(DIFF_HEAD_DIM), small(DIFF_HEAD_DIM),
                  small(DIFF_V_DIM)],
        out_specs=pl.BlockSpec((None, seq, DIFF_V_DIM), lambda h, b: (b, 0, h)),
        out_shape=jax.ShapeDtypeStruct((bsz, seq, DIFF_HEADS * DIFF_V_DIM), BF16),
        scratch_shapes=[pltpu.VMEM((n_tiles, DIFF_V_DIM + ATT_ONES_ROWS, ATT_T), BF16),
                        pltpu.VMEM((ATT_T, ATT_T), F32),
                        pltpu.VMEM((ATT_T, ATT_FEAT), BF16),
                        pltpu.VMEM((n_tiles, DIFF_V_DIM + ATT_FEAT, 2 * ATT_T), BF16),
                        pltpu.VMEM((ATT_T, 2 * ATT_T), F32), pltpu.VMEM((ATT_T, 2 * ATT_T), F32),
                        pltpu.VMEM((1, 2 * ATT_T), F32), pltpu.VMEM((1, 2 * ATT_T), F32),
                        pltpu.VMEM((n_tiles, 1, 2 * ATT_T), F32),
                        pltpu.VMEM((n_tiles, DIFF_V_DIM + ATT_ONES_ROWS, 2 * ATT_T), F32)],
        compiler_params=_params("arbitrary", "arbitrary"), name="diff_attention",
    )(slopes, tasks, qkv, qkv, qkv, vec(lq1), vec(lk1), vec(lq2), vec(lk2), vec(subln_w))


def kernel(x, even_w_in, even_conv_w, even_conv_b, even_dt_bias, even_a_log, even_d, even_ssd_norm_w, even_sgu_ln_g, even_sgu_ln_b, even_sgu_w, even_sgu_b, even_w_out, odd_w_qkv, odd_lambda_q1, odd_lambda_k1, odd_lambda_q2, odd_lambda_k2, odd_subln_w, odd_w_out, ln_mix_g, ln_mix_b, ln_ffn_g, ln_ffn_b, mlp_w_up, mlp_w_down):
    bsz, seq, d = x.shape
    h = x.reshape(bsz * seq, d)
    row = lambda p: p.astype(F32).reshape(1, -1)
    for l in range(DEPTH):
        i = l // 2
        if l % 2 == 0:
            w_in = even_w_in[i]
            o_z, o_xbc, o_dt, o_sgu = 0, SSD_INNER, SSD_INNER + SSD_CONV_DIM, SSD_INNER + SSD_CONV_DIM + SSD_HEADS
            w_main = jnp.concatenate([w_in[:, o_z:o_xbc], w_in[:, o_sgu:], w_in[:, o_xbc:o_dt]], axis=1).astype(BF16)
            w_dt = jnp.pad(w_in[:, o_dt:o_sgu], ((0, 0), (0, DT_PAD - SSD_HEADS))).astype(BF16)
            proj, dt_raw = _in_proj(h, w_main, w_dt)
            y_a = _ssd_mixer(proj, dt_raw, even_conv_w[i], even_conv_b[i], even_dt_bias[i], even_a_log[i],
                             even_d[i], even_ssd_norm_w[i], bsz, seq)
            y_b = _sgu_mixer(proj, even_sgu_ln_g[i], even_sgu_ln_b[i], even_sgu_w[i], even_sgu_b[i])
            w_out = even_w_out[i].astype(BF16)
            h = _proj_ln([y_a, y_b], [w_out[:SSD_INNER], w_out[SSD_INNER:]], h, row(ln_mix_g[l]), row(ln_mix_b[l]),
                         "even_out_proj_ln")
        else:
            lambda_init = 0.8 - 0.6 * math.exp(-0.3 * l)
            w_qkv = odd_w_qkv[i]
            q_scale = DIFF_HEAD_DIM ** -0.5 * LOG2_E
            w_qkv = jnp.concatenate([w_qkv[:, :DIFF_QK] * q_scale, w_qkv[:, DIFF_QK:]], axis=1).astype(BF16)
            qkv = _matmul(h, w_qkv, PROJ_TM, BF16, "qkv_proj").reshape(bsz, seq, -1)
            o = _diff_attention(qkv, odd_lambda_q1[i], odd_lambda_k1[i], odd_lambda_q2[i], odd_lambda_k2[i],
                                odd_subln_w[i], lambda_init, bsz, seq)
            h = _proj_ln([o.reshape(bsz * seq, -1)], [odd_w_out[i].astype(BF16)], h, row(ln_mix_g[l]), row(ln_mix_b[l]),
                         "odd_out_proj_ln")
        h = _mlp_ln(h, mlp_w_up[l].astype(BF16), mlp_w_down[l].astype(BF16), row(ln_ffn_g[l]), row(ln_ffn_b[l]),
                    "mlp_ln_%d" % l)
    return h.reshape(bsz, seq, d)
```

```python
import functools
import math

import jax
import jax.numpy as jnp
from jax import lax
from jax.experimental import pallas as pl
from jax.experimental.pallas import tpu as pltpu

F32 = jnp.float32
BF16 = jnp.bfloat16

D_MODEL = 1024
DEPTH = 2
CHUNK = 64
SSD_HEADS = 16
SSD_HEAD_DIM = 64
SSD_INNER = SSD_HEADS * SSD_HEAD_DIM
SSD_GROUPS = 2
SSD_STATE = 128
SSD_CONV = 4
SSD_BC = SSD_GROUPS * SSD_STATE
SSD_CONV_DIM = SSD_INNER + 2 * SSD_BC
SSD_GROUP_WIDTH = SSD_INNER // SSD_GROUPS
SGU_BLOCK = 128
SGU_GROUPS = 8
SGU_WIDTH = 1024
SGU_GROUP_DIM = SGU_WIDTH // SGU_GROUPS
DIFF_HEADS = 8
DIFF_HEAD_DIM = 64
DIFF_V_DIM = 2 * DIFF_HEAD_DIM
DIFF_QK = DIFF_HEADS * 2 * DIFF_HEAD_DIM
D_FF = 4 * D_MODEL
DEEPNORM_ALPHA = (2 * DEPTH) ** 0.25
LN_EPS = 1e-5

V7X_LANES = 128
V7X_SUBLANES = 8
V7X_VMEM_BYTES = 64 * 1024 * 1024
VMEM_LIMIT = 56 * 1024 * 1024

PROJ_TM = 1024
IN_PROJ_TM = 512
PROJ_TN = 1536
LN_TM = 1024
MLP_TM = 1024
MLP_FF_CHUNK = 1024
SSD_L = 256
SSD_CHUNK = 128
SGU_TM = 1024
ATT_T = 512
ATT_ONES_ROWS = 16
ATT_FEAT = V7X_LANES
ATT_TASK_FIELDS = 2
ATT_UNROLL = 4
LOG2_E = 1.0 / math.log(2.0)
DT_PAD = V7X_LANES


def _params(*sem):
    return pltpu.CompilerParams(dimension_semantics=sem, vmem_limit_bytes=VMEM_LIMIT)


def _layer_norm(y, g, b):
    mu = jnp.mean(y, axis=-1, keepdims=True)
    yc = y - mu
    var = jnp.mean(yc * yc, axis=-1, keepdims=True)
    return yc * lax.rsqrt(var + LN_EPS) * g + b


def _dot(a, b):
    return jnp.dot(a, b, preferred_element_type=F32)


def _dot_nt(a, b):
    return lax.dot_general(a, b, (((1,), (1,)), ((), ())), preferred_element_type=F32)


def _project(x_ref, w_ref, o_ref):
    xb = x_ref[...].astype(BF16)
    for c in range(w_ref.shape[1] // PROJ_TN):
        cols = slice(c * PROJ_TN, (c + 1) * PROJ_TN)
        o_ref[:, cols] = _dot(xb, w_ref[:, cols]).astype(o_ref.dtype)
    return xb


def _matmul_kernel(x_ref, w_ref, o_ref):
    _project(x_ref, w_ref, o_ref)


def _matmul(x, w, tm, out_dtype, name):
    t, k = x.shape
    n = w.shape[1]
    return pl.pallas_call(
        _matmul_kernel, grid=(t // tm,),
        in_specs=[pl.BlockSpec((tm, k), lambda i: (i, 0)), pl.BlockSpec((k, n), lambda i: (0, 0))],
        out_specs=pl.BlockSpec((tm, n), lambda i: (i, 0)),
        out_shape=jax.ShapeDtypeStruct((t, n), out_dtype),
        compiler_params=_params("parallel"), name=name)(x, w)


def _silu(x):
    h = 0.5 * x
    return h + h * jnp.tanh(h)


def _gelu(x):
    return 0.5 * x * (1.0 + lax.erf(x * (1.0 / math.sqrt(2.0))))


def _in_proj_kernel(x_ref, w_ref, wdt_ref, o_ref, dt_ref):
    xb = _project(x_ref, w_ref, o_ref)
    dt_ref[...] = _dot(xb, wdt_ref[...])


def _in_proj(x, w, w_dt):
    t, k = x.shape
    n = w.shape[1]
    tm = IN_PROJ_TM
    return pl.pallas_call(
        _in_proj_kernel, grid=(t // tm,),
        in_specs=[pl.BlockSpec((tm, k), lambda i: (i, 0)), pl.BlockSpec((k, n), lambda i: (0, 0)),
                  pl.BlockSpec((k, DT_PAD), lambda i: (0, 0))],
        out_specs=[pl.BlockSpec((tm, n), lambda i: (i, 0)), pl.BlockSpec((tm, DT_PAD), lambda i: (i, 0))],
        out_shape=[jax.ShapeDtypeStruct((t, n), BF16), jax.ShapeDtypeStruct((t, DT_PAD), F32)],
        compiler_params=_params("parallel"), name="in_proj")(x, w, w_dt)


def _proj_ln_kernel(*refs, n_in):
    x_refs, w_refs = refs[:n_in], refs[n_in:2 * n_in]
    h_ref, g_ref, b_ref, o_ref = refs[2 * n_in:]
    acc = _dot(jnp.concatenate([x_ref[...] for x_ref in x_refs], axis=1),
               jnp.concatenate([w_ref[...] for w_ref in w_refs], axis=0))
    o_ref[...] = _layer_norm(DEEPNORM_ALPHA * h_ref[...] + acc, g_ref[...], b_ref[...])


def _proj_ln(xs, ws, h, g, b, name):
    t, d = h.shape
    tm = LN_TM
    n_in = len(xs)
    in_specs = ([pl.BlockSpec((tm, x.shape[1]), lambda i: (i, 0)) for x in xs]
                + [pl.BlockSpec(w.shape, lambda i: (0, 0)) for w in ws]
                + [pl.BlockSpec((tm, d), lambda i: (i, 0)),
                   pl.BlockSpec((1, d), lambda i: (0, 0)), pl.BlockSpec((1, d), lambda i: (0, 0))])
    return pl.pallas_call(
        functools.partial(_proj_ln_kernel, n_in=n_in), grid=(t // tm,), in_specs=in_specs,
        out_specs=pl.BlockSpec((tm, d), lambda i: (i, 0)),
        out_shape=jax.ShapeDtypeStruct((t, d), F32),
        compiler_params=_params("parallel"), name=name)(*xs, *ws, h, g, b)


def _mlp_ln_kernel(h_ref, wu_ref, wd_ref, g_ref, b_ref, o_ref):
    h = h_ref[...]
    hb = h.astype(BF16)
    acc = jnp.zeros(h.shape, F32)
    for c in range(D_FF // MLP_FF_CHUNK):
        cols = slice(c * MLP_FF_CHUNK, (c + 1) * MLP_FF_CHUNK)
        u = jnp.maximum(_dot(hb, wu_ref[:, cols]), 0.0)
        acc += _dot((u * u).astype(BF16), wd_ref[cols, :])
    o_ref[...] = _layer_norm(DEEPNORM_ALPHA * h + acc, g_ref[...], b_ref[...])


def _mlp_ln(h, w_up, w_down, g, b, name):
    t, d = h.shape
    tm = MLP_TM
    return pl.pallas_call(
        _mlp_ln_kernel, grid=(t // tm,),
        in_specs=[pl.BlockSpec((tm, d), lambda i: (i, 0)),
                  pl.BlockSpec(w_up.shape, lambda i: (0, 0), pipeline_mode=pl.Buffered(1)),
                  pl.BlockSpec(w_down.shape, lambda i: (0, 0), pipeline_mode=pl.Buffered(1)),
                  pl.BlockSpec((1, d), lambda i: (0, 0)), pl.BlockSpec((1, d), lambda i: (0, 0))],
        out_specs=pl.BlockSpec((tm, d), lambda i: (i, 0)),
        out_shape=jax.ShapeDtypeStruct((t, d), F32),
        compiler_params=_params("parallel"), name=name)(h, w_up, w_down, g, b)


def _softplus(x):
    return jnp.maximum(x, 0.0) + jnp.log1p(jnp.exp(-jnp.abs(x)))


def _cumsum_rows(x, segment):
    row = lax.broadcasted_iota(jnp.int32, x.shape, 0) % segment
    shift = 1
    while shift < segment:
        x = x + jnp.where(row >= shift, pltpu.roll(x, shift, 0), 0.0)
        shift *= 2
    return x


def _ssd_kernel(z_ref, xbc_ref, dtr_ref, shift_ref, cw_ref, cb_ref, dtb_ref, alog_ref, dskip_ref, nw_ref, e_ref,
                o_ref, tail_ref, state_ref):
    L, C = SSD_L, SSD_CHUNK
    sub = V7X_SUBLANES

    @pl.when(pl.program_id(1) == 0)
    def _():
        tail_ref[...] = jnp.zeros(tail_ref.shape, F32)
        state_ref[...] = jnp.zeros(state_ref.shape, F32)

    x_b = xbc_ref[...]
    x_f = x_b.astype(F32)
    conv = cb_ref[...] + cw_ref[SSD_CONV - 1:SSD_CONV, :] * x_f
    tail = tail_ref[...]
    row8 = lax.broadcasted_iota(jnp.int32, tail.shape, 0)
    head_fix = jnp.zeros(tail.shape, F32)
    for d in range(1, SSD_CONV):
        w = cw_ref[SSD_CONV - 1 - d:SSD_CONV - d, :]
        conv += w * _dot(shift_ref[d - 1], x_b)
        head_fix += w * jnp.where(row8 < d, pltpu.roll(tail, d, 0), 0.0)
    conv = jnp.concatenate([conv[:sub] + head_fix, conv[sub:]], axis=0)
    tail_ref[...] = x_f[L - sub:, :]
    xbc = _silu(conv)
    xs = xbc[:, :SSD_INNER]
    b_b = xbc[:, SSD_INNER:SSD_INNER + SSD_BC].astype(BF16)
    c_b = xbc[:, SSD_INNER + SSD_BC:].astype(BF16)

    dt = _softplus(dtr_ref[...] + dtb_ref[...])
    a_cs = _cumsum_rows(dt * (-LOG2_E * jnp.exp(alog_ref[...])), C)
    a_end = jnp.concatenate([jnp.broadcast_to(a_cs[(c + 1) * C - 1:(c + 1) * C, :], (C, DT_PAD))
                             for c in range(L // C)], axis=0)
    exp_a = jnp.exp2(a_cs)
    to_end = jnp.exp2(a_end - a_cs)
    stacked = jnp.concatenate([dt, exp_a, dt * to_end], axis=0)
    hi = stacked.astype(BF16)
    lo = (stacked - hi.astype(F32)).astype(BF16)
    expanded = _dot(jnp.concatenate([hi, lo], axis=1), e_ref[...])
    dt_e, exp_a_e, w_end_e = expanded[:L], expanded[L:2 * L], expanded[2 * L:]

    xdt_b = (xs * dt_e).astype(BF16)
    xend_b = (xs * w_end_e).astype(BF16)
    causal = (lax.broadcasted_iota(jnp.int32, (C, C), 0) >= lax.broadcasted_iota(jnp.int32, (C, C), 1))
    lane = lax.broadcasted_iota(jnp.int32, (C, V7X_LANES), 1)
    heads_per_group = SSD_HEADS // SSD_GROUPS

    y_chunks = []
    for c in range(L // C):
        rows = slice(c * C, (c + 1) * C)
        a_c = a_cs[rows]
        a_c_t = a_c.T
        y_parts = []
        for g in range(SSD_GROUPS):
            gs = slice(g * SSD_STATE, (g + 1) * SSD_STATE)
            gw = slice(g * SSD_GROUP_WIDTH, (g + 1) * SSD_GROUP_WIDTH)
            cb = _dot_nt(c_b[rows, gs], b_b[rows, gs])
            y_off = _dot(c_b[rows, gs], state_ref[g].astype(BF16)) * exp_a_e[rows, gw]
            for pair in range(heads_per_group // 2):
                h0 = g * heads_per_group + 2 * pair
                x_pair = xdt_b[rows, h0 * SSD_HEAD_DIM:(h0 + 2) * SSD_HEAD_DIM]
                zero = jnp.zeros_like(x_pair)
                mixes = []
                for h in (h0, h0 + 1):
                    seg = a_c[:, h:h + 1] - a_c_t[h:h + 1, :]
                    mixes.append((cb * jnp.exp2(jnp.where(causal, seg, -jnp.inf))).astype(BF16))
                rhs = jnp.concatenate([jnp.where(lane < SSD_HEAD_DIM, x_pair, zero),
                                       jnp.where(lane >= SSD_HEAD_DIM, x_pair, zero)], axis=0)
                y_pair = _dot(jnp.concatenate(mixes, axis=1), rhs)
                y_parts.append(y_pair + y_off[:, 2 * pair * SSD_HEAD_DIM:(2 * pair + 2) * SSD_HEAD_DIM])
            upd = lax.dot_general(b_b[rows, gs], xend_b[rows, gw], (((0,), (0,)), ((), ())),
                                  preferred_element_type=F32)
            state_ref[g] = state_ref[g] * exp_a_e[(c + 1) * C - 1:(c + 1) * C, gw] + upd
        y_chunks.append(jnp.concatenate(y_parts, axis=1))

    y = jnp.concatenate(y_chunks, axis=0) + xs * dskip_ref[...]
    y = y * _silu(z_ref[...].astype(F32))
    normed = []
    for g in range(SSD_GROUPS):
        yg = y[:, g * SSD_GROUP_WIDTH:(g + 1) * SSD_GROUP_WIDTH]
        normed.append(yg * lax.rsqrt(jnp.mean(yg * yg, axis=-1, keepdims=True) + LN_EPS))
    o_ref[...] = (jnp.concatenate(normed, axis=1) * nw_ref[...]).astype(o_ref.dtype)


def _ssd_mixer(proj, dt_raw, conv_w, conv_b, dt_bias, a_log, d_skip, norm_w, bsz, seq):
    steps = seq // SSD_L
    row = lambda b, s: b * steps + s
    pad16 = lambda v: jnp.pad(v.astype(F32), (0, DT_PAD - SSD_HEADS)).reshape(1, DT_PAD)
    head_of_channel = jnp.arange(SSD_INNER) // SSD_HEAD_DIM
    expand = (jnp.arange(DT_PAD)[:, None] == head_of_channel[None, :]).astype(BF16)
    expand = jnp.concatenate([expand, expand], axis=0)
    t_idx = jnp.arange(SSD_L)
    shifts = jnp.stack([(t_idx[:, None] - d == t_idx[None, :]) for d in range(1, SSD_CONV)]).astype(BF16)
    d_e = jnp.repeat(d_skip.astype(F32), SSD_HEAD_DIM).reshape(1, SSD_INNER)
    const = lambda shape: pl.BlockSpec(shape, lambda b, s: (0,) * len(shape))
    return pl.pallas_call(
        _ssd_kernel, grid=(bsz, steps),
        in_specs=[pl.BlockSpec((SSD_L, SSD_INNER), lambda b, s: (row(b, s), 0)),
                  pl.BlockSpec((SSD_L, SSD_CONV_DIM), lambda b, s: (row(b, s), 2)),
                  pl.BlockSpec((SSD_L, DT_PAD), lambda b, s: (row(b, s), 0)),
                  const((SSD_CONV - 1, SSD_L, SSD_L)),
                  const((SSD_CONV, SSD_CONV_DIM)), const((1, SSD_CONV_DIM)),
                  const((1, DT_PAD)), const((1, DT_PAD)),
                  const((1, SSD_INNER)), const((1, SSD_INNER)), const((2 * DT_PAD, SSD_INNER))],
        out_specs=pl.BlockSpec((SSD_L, SSD_INNER), lambda b, s: (row(b, s), 0)),
        out_shape=jax.ShapeDtypeStruct((bsz * seq, SSD_INNER), BF16),
        scratch_shapes=[pltpu.VMEM((V7X_SUBLANES, SSD_CONV_DIM), F32),
                        pltpu.VMEM((SSD_GROUPS, SSD_STATE, SSD_GROUP_WIDTH), F32)],
        compiler_params=_params("parallel", "arbitrary"), name="ssd_mixer",
    )(proj, proj, dt_raw, shifts, conv_w.astype(F32), conv_b.astype(F32).reshape(1, -1), pad16(dt_bias),
      pad16(a_log), d_e, norm_w.astype(F32).reshape(1, -1), expand)


def _sgu_kernel(u_ref, v_ref, g_ref, b_ref, w_ref, bs_ref, o_ref):
    v = _layer_norm(_gelu(v_ref[...].astype(F32)), g_ref[...], b_ref[...]).astype(BF16)
    r = lax.broadcasted_iota(jnp.int32, (SGU_BLOCK, SGU_BLOCK), 0) // CHUNK
    c = lax.broadcasted_iota(jnp.int32, (SGU_BLOCK, SGU_BLOCK), 1) // CHUNK
    for g in range(SGU_GROUPS):
        w = jnp.where(r >= c, w_ref[g], 0.0).astype(BF16)
        cols = slice(g * SGU_GROUP_DIM, (g + 1) * SGU_GROUP_DIM)
        for blk in range(SGU_TM // SGU_BLOCK):
            rows = slice(blk * SGU_BLOCK, (blk + 1) * SGU_BLOCK)
            mixed = _dot(w, v[rows, cols]) + bs_ref[g]
            o_ref[rows, cols] = (_gelu(u_ref[rows, cols].astype(F32)) * mixed).astype(o_ref.dtype)


def _sgu_mixer(proj, ln_g, ln_b, w_s, b_s):
    t = proj.shape[0]
    bias = jnp.broadcast_to(b_s.astype(F32)[:, :, None], (SGU_GROUPS, SGU_BLOCK, SGU_GROUP_DIM))
    return pl.pallas_call(
        _sgu_kernel, grid=(t // SGU_TM,),
        in_specs=[pl.BlockSpec((SGU_TM, SGU_WIDTH), lambda i: (i, 1)),
                  pl.BlockSpec((SGU_TM, SGU_WIDTH), lambda i: (i, 2)),
                  pl.BlockSpec((1, SGU_WIDTH), lambda i: (0, 0)), pl.BlockSpec((1, SGU_WIDTH), lambda i: (0, 0)),
                  pl.BlockSpec((SGU_GROUPS, SGU_BLOCK, SGU_BLOCK), lambda i: (0, 0, 0)),
                  pl.BlockSpec((SGU_GROUPS, SGU_BLOCK, SGU_GROUP_DIM), lambda i: (0, 0, 0))],
        out_specs=pl.BlockSpec((SGU_TM, SGU_WIDTH), lambda i: (i, 0)),
        out_shape=jax.ShapeDtypeStruct((t, SGU_WIDTH), BF16),
        compiler_params=_params("parallel"), name="sgu_mixer",
    )(proj, proj, ln_g.astype(F32).reshape(1, -1), ln_b.astype(F32).reshape(1, -1), w_s.astype(F32), bias)


def _attn_tasks(n_tiles):
    rows = [(i, i) for i in range(n_tiles)] + [(i, j) for i in range(n_tiles) for j in range(i)]
    return len(rows), jnp.asarray(rows, jnp.int32).reshape(-1)


def _attn_kernel(slopes_ref, tasks_ref, q_ref, k_ref, v_ref, lq1_ref, lk1_ref, lq2_ref, lk2_ref, sw_ref, o_ref,
                 vt_ref, dbias_ref, kfeat_ref, qq_ref, s0_ref, s1_ref, cmax0_ref, cmax1_ref, m_ref, acc_ref,
                 *, lambda_init, n_tasks, n_tiles):
    t = ATT_T
    s_refs, cmax_refs = (s0_ref, s1_ref), (cmax0_ref, cmax1_ref)
    head, b = pl.program_id(0), pl.program_id(1)
    slope = slopes_ref[head]
    field = lambda n, f: tasks_ref[ATT_TASK_FIELDS * n + f]

    @pl.when(b == 0)
    def _():
        feat = lax.broadcasted_iota(jnp.int32, (ATT_FEAT, 2 * t), 0)
        for i in range(n_tiles):
            qq_ref[i, 2 * DIFF_HEAD_DIM:, :] = jnp.where(feat < 3, 1.0, 0.0).astype(F32).astype(BF16)
        key = lax.broadcasted_iota(jnp.int32, (t, t), 0)
        qry = lax.broadcasted_iota(jnp.int32, (t, t), 1)
        dbias_ref[...] = jnp.where(key // CHUNK <= qry // CHUNK,
                                   (-2.0 * slope) * jnp.maximum(key - qry, 0).astype(F32), -jnp.inf)
        bias = slope * lax.broadcasted_iota(jnp.int32, (t, ATT_FEAT), 0).astype(F32)
        hi = bias.astype(BF16).astype(F32)
        mid = (bias - hi).astype(BF16).astype(F32)
        lo = bias - hi - mid
        col = lax.broadcasted_iota(jnp.int32, (t, ATT_FEAT), 1)
        kfeat_ref[...] = jnp.where(col == 0, hi, jnp.where(col == 1, mid, jnp.where(col == 2, lo, 0.0))).astype(BF16)

    lam = (jnp.exp(jnp.sum(lq1_ref[...] * lk1_ref[...], axis=-1, keepdims=True))
           - jnp.exp(jnp.sum(lq2_ref[...] * lk2_ref[...], axis=-1, keepdims=True)) + lambda_init)

    ones = jnp.ones((ATT_ONES_ROWS, t), BF16)
    for c in range(n_tiles):
        vt_ref[c, :DIFF_V_DIM, :] = v_ref[c * t:(c + 1) * t, :].astype(F32).T.astype(BF16)
        vt_ref[c, DIFF_V_DIM:, :] = ones
    for i in range(n_tiles):
        q_t = q_ref[i * t:(i + 1) * t, :].astype(F32).T
        feat = lax.broadcasted_iota(jnp.int32, q_t.shape, 0)
        qq_ref[i, :2 * DIFF_HEAD_DIM, :t] = jnp.where(feat < DIFF_HEAD_DIM, q_t, 0.0).astype(BF16)
        qq_ref[i, :2 * DIFF_HEAD_DIM, t:] = jnp.where(feat >= DIFF_HEAD_DIM, q_t, 0.0).astype(BF16)

    half = t // 2
    late_queries = (slice(half, t), slice(t + half, 2 * t))
    quadrant = lambda late_cols, fill: jnp.concatenate(
        [jnp.full((late_cols[0].shape[0], half), fill, F32), late_cols[0],
         jnp.full((late_cols[1].shape[0], half), fill, F32), late_cols[1]], axis=1)

    def scores(n, slot, diagonal):
        i, j = field(n, 0), field(n, 1)
        lhs = jnp.concatenate([k_ref[pl.ds(pl.multiple_of(j * t, t), t), :], kfeat_ref[...]], axis=1)
        if diagonal:
            dbias = dbias_ref[...]
            early = _dot(lhs[:half], qq_ref[i]) + jnp.concatenate([dbias[:half], dbias[:half]], axis=1)
            late = [_dot(lhs[half:], qq_ref[i, :, cols]) + dbias[half:, half:] for cols in late_queries]
            s = jnp.concatenate([early, quadrant(late, -jnp.inf)], axis=0)
        else:
            s = _dot(lhs, qq_ref[i])
        s_refs[slot][...] = s
        cmax_refs[slot][...] = jnp.max(s, axis=0, keepdims=True)

    def accumulate(n, slot, diagonal):
        i, j = field(n, 0), field(n, 1)
        if diagonal:
            m_new = cmax_refs[slot][...]
            p = jnp.exp2(s_refs[slot][...] - m_new).astype(BF16)
            late = [_dot(vt_ref[j, :, half:], p[half:, cols]) for cols in late_queries]
            acc_ref[i] = _dot(vt_ref[j, :, :half], p[:half]) + quadrant(late, 0.0)
        else:
            shift = slope * ((j - i) * t).astype(F32)
            m = m_ref[i]
            m_new = jnp.maximum(m, cmax_refs[slot][...] + shift)
            alpha = jnp.exp2(m - m_new)
            p = jnp.exp2(s_refs[slot][...] - (m_new - shift)).astype(BF16)
            acc_ref[i] = alpha * acc_ref[i] + _dot(vt_ref[j], p)
        m_ref[i] = m_new

    def stage(n, scores_diagonal, consume_diagonal):
        scores(n + 1, 1 - n % 2, scores_diagonal)
        accumulate(n, n % 2, consume_diagonal)

    def finish(i):
        o = acc_ref[i, :DIFF_V_DIM, :] * (1.0 / acc_ref[i, DIFF_V_DIM:DIFF_V_DIM + 1, :])
        o = o[:, :t] - lam * o[:, t:]
        o = o * lax.rsqrt(jnp.mean(o * o, axis=0, keepdims=True) + LN_EPS)
        o_ref[i * t:(i + 1) * t, :] = (o.T * sw_ref[...] * (1.0 - lambda_init)).astype(o_ref.dtype)

    def run_stages(first, stop, diagonal):
        if first < stop and first % 2 == 1:
            stage(first, diagonal, diagonal)
            first += 1
        rounds = (stop - first) // ATT_UNROLL

        def body(r, carry):
            n = first + ATT_UNROLL * r
            for u in range(ATT_UNROLL):
                scores(n + u + 1, (u + 1) % 2, diagonal)
                accumulate(n + u, u % 2, diagonal)
            return carry

        if rounds > 0:
            lax.fori_loop(0, rounds, body, 0)
        for n in range(first + ATT_UNROLL * rounds, stop):
            stage(n, diagonal, diagonal)

    scores(0, 0, diagonal=True)
    run_stages(0, n_tiles - 1, diagonal=True)
    if n_tasks > n_tiles:
        stage(n_tiles - 1, False, True)
        run_stages(n_tiles, n_tasks - 1, diagonal=False)
    accumulate(n_tasks - 1, (n_tasks - 1) % 2, diagonal=n_tasks == n_tiles)
    for i in range(n_tiles):
        finish(i)


def _diff_attention(qkv, lq1, lk1, lq2, lk2, subln_w, lambda_init, bsz, seq):
    slopes = LOG2_E * jnp.exp2(-8.0 * jnp.arange(1, DIFF_HEADS + 1, dtype=F32) / DIFF_HEADS)
    n_tiles = seq // ATT_T
    n_tasks, tasks = _attn_tasks(n_tiles)
    vec = lambda p: p.astype(F32).reshape(1, -1)
    head_blk = 2 * DIFF_HEAD_DIM
    small = lambda n: pl.BlockSpec((1, n), lambda h, b: (0, 0))
    smem = pl.BlockSpec(memory_space=pltpu.SMEM)
    return pl.pallas_call(
        functools.partial(_attn_kernel, lambda_init=lambda_init, n_tasks=n_tasks, n_tiles=n_tiles),
        grid=(DIFF_HEADS, bsz),
        in_specs=[smem, smem,
                  pl.BlockSpec((None, seq, head_blk), lambda h, b: (b, 0, h)),
                  pl.BlockSpec((None, seq, head_blk), lambda h, b: (b, 0, DIFF_HEADS + h)),
                  pl.BlockSpec((None, seq, DIFF_V_DIM), lambda h, b: (b, 0, 2 * DIFF_HEADS + h)),
                  small(DIFF_HEAD_DIM), small(DIFF_HEAD_DIM), small(DIFF_HEAD_DIM), small(DIFF_HEAD_DIM),
                  small(DIFF_V_DIM)],
        out_specs=pl.BlockSpec((None, seq, DIFF_V_DIM), lambda h, b: (b, 0, h)),
        out_shape=jax.ShapeDtypeStruct((bsz, seq, DIFF_HEADS * DIFF_V_DIM), BF16),
        scratch_shapes=[pltpu.VMEM((n_tiles, DIFF_V_DIM + ATT_ONES_ROWS, ATT_T), BF16),
                        pltpu.VMEM((ATT_T, ATT_T), F32),
                        pltpu.VMEM((ATT_T, ATT_FEAT), BF16),
                        pltpu.VMEM((n_tiles, DIFF_V_DIM + ATT_FEAT, 2 * ATT_T), BF16),
                        pltpu.VMEM((ATT_T, 2 * ATT_T), F32), pltpu.VMEM((ATT_T, 2 * ATT_T), F32),
                        pltpu.VMEM((1, 2 * ATT_T), F32), pltpu.VMEM((1, 2 * ATT_T), F32),
                        pltpu.VMEM((n_tiles, 1, 2 * ATT_T), F32),
                        pltpu.VMEM((n_tiles, DIFF_V_DIM + ATT_ONES_ROWS, 2 * ATT_T), F32)],
        compiler_params=_params("arbitrary", "arbitrary"), name="diff_attention",
    )(slopes, tasks, qkv, qkv, qkv, vec(lq1), vec(lk1), vec(lq2), vec(lk2), vec(subln_w))


def kernel(x, even_w_in, even_conv_w, even_conv_b, even_dt_bias, even_a_log, even_d, even_ssd_norm_w, even_sgu_ln_g, even_sgu_ln_b, even_sgu_w, even_sgu_b, even_w_out, odd_w_qkv, odd_lambda_q1, odd_lambda_k1, odd_lambda_q2, odd_lambda_k2, odd_subln_w, odd_w_out, ln_mix_g, ln_mix_b, ln_ffn_g, ln_ffn_b, mlp_w_up, mlp_w_down):
    bsz, seq, d = x.shape
    h = x.reshape(bsz * seq, d)
    row = lambda p: p.astype(F32).reshape(1, -1)
    for l in range(DEPTH):
        i = l // 2
        if l % 2 == 0:
            w_in = even_w_in[i]
            o_z, o_xbc, o_dt, o_sgu = 0, SSD_INNER, SSD_INNER + SSD_CONV_DIM, SSD_INNER + SSD_CONV_DIM + SSD_HEADS
            w_main = jnp.concatenate([w_in[:, o_z:o_xbc], w_in[:, o_sgu:], w_in[:, o_xbc:o_dt]], axis=1).astype(BF16)
            w_dt = jnp.pad(w_in[:, o_dt:o_sgu], ((0, 0), (0, DT_PAD - SSD_HEADS))).astype(BF16)
            proj, dt_raw = _in_proj(h, w_main, w_dt)
            y_a = _ssd_mixer(proj, dt_raw, even_conv_w[i], even_conv_b[i], even_dt_bias[i], even_a_log[i],
                             even_d[i], even_ssd_norm_w[i], bsz, seq)
            y_b = _sgu_mixer(proj, even_sgu_ln_g[i], even_sgu_ln_b[i], even_sgu_w[i], even_sgu_b[i])
            w_out = even_w_out[i].astype(BF16)
            h = _proj_ln([y_a, y_b], [w_out[:SSD_INNER], w_out[SSD_INNER:]], h, row(ln_mix_g[l]), row(ln_mix_b[l]),
                         "even_out_proj_ln")
        else:
            lambda_init = 0.8 - 0.6 * math.exp(-0.3 * l)
            w_qkv = odd_w_qkv[i]
            q_scale = DIFF_HEAD_DIM ** -0.5 * LOG2_E
            w_qkv = jnp.concatenate([w_qkv[:, :DIFF_QK] * q_scale, w_qkv[:, DIFF_QK:]], axis=1).astype(BF16)
            qkv = _matmul(h, w_qkv, PROJ_TM, BF16, "qkv_proj").reshape(bsz, seq, -1)
            o = _diff_attention(qkv, odd_lambda_q1[i], odd_lambda_k1[i], odd_lambda_q2[i], odd_lambda_k2[i],
                                odd_subln_w[i], lambda_init, bsz, seq)
            h = _proj_ln([o.reshape(bsz * seq, -1)], [odd_w_out[i].astype(BF16)], h, row(ln_mix_g[l]), row(ln_mix_b[l]),
                         "odd_out_proj_ln")
        h = _mlp_ln(h, mlp_w_up[l].astype(BF16), mlp_w_down[l].astype(BF16), row(ln_ffn_g[l]), row(ln_ffn_b[l]),
                    "mlp_ln_%d" % l)
    return h.reshape(bsz, seq, d)
```

```python
import functools
import math

import jax
import jax.numpy as jnp
from jax import lax
from jax.experimental import pallas as pl
from jax.experimental.pallas import tpu as pltpu

F32 = jnp.float32
BF16 = jnp.bfloat16

D_MODEL = 1024
DEPTH = 2
CHUNK = 64
SSD_HEADS = 16
SSD_HEAD_DIM = 64
SSD_INNER = SSD_HEADS * SSD_HEAD_DIM
SSD_GROUPS = 2
SSD_STATE = 128
SSD_CONV = 4
SSD_BC = SSD_GROUPS * SSD_STATE
SSD_CONV_DIM = SSD_INNER + 2 * SSD_BC
SSD_GROUP_WIDTH = SSD_INNER // SSD_GROUPS
SGU_BLOCK = 128
SGU_GROUPS = 8
SGU_WIDTH = 1024
SGU_GROUP_DIM = SGU_WIDTH // SGU_GROUPS
DIFF_HEADS = 8
DIFF_HEAD_DIM = 64
DIFF_V_DIM = 2 * DIFF_HEAD_DIM
DIFF_QK = DIFF_HEADS * 2 * DIFF_HEAD_DIM
D_FF = 4 * D_MODEL
DEEPNORM_ALPHA = (2 * DEPTH) ** 0.25
LN_EPS = 1e-5

V7X_LANES = 128
V7X_SUBLANES = 8
V7X_VMEM_BYTES = 64 * 1024 * 1024
VMEM_LIMIT = 56 * 1024 * 1024

PROJ_TM = 1024
IN_PROJ_TM = 512
PROJ_TN = 1536
LN_TM = 1024
MLP_TM = 1024
MLP_FF_CHUNK = 1024
SSD_L = 256
SSD_CHUNK = 128
SGU_TM = 1024
ATT_T = 512
ATT_ONES_ROWS = 16
ATT_FEAT = V7X_LANES
ATT_TASK_FIELDS = 2
ATT_UNROLL = 8
LOG2_E = 1.0 / math.log(2.0)
DT_PAD = V7X_LANES


def _params(*sem):
    return pltpu.CompilerParams(dimension_semantics=sem, vmem_limit_bytes=VMEM_LIMIT)


def _layer_norm(y, g, b):
    mu = jnp.mean(y, axis=-1, keepdims=True)
    yc = y - mu
    var = jnp.mean(yc * yc, axis=-1, keepdims=True)
    return yc * lax.rsqrt(var + LN_EPS) * g + b


def _dot(a, b):
    return jnp.dot(a, b, preferred_element_type=F32)


def _dot_nt(a, b):
    return lax.dot_general(a, b, (((1,), (1,)), ((), ())), preferred_element_type=F32)


def _project(x_ref, w_ref, o_ref):
    xb = x_ref[...].astype(BF16)
    for c in range(w_ref.shape[1] // PROJ_TN):
        cols = slice(c * PROJ_TN, (c + 1) * PROJ_TN)
        o_ref[:, cols] = _dot(xb, w_ref[:, cols]).astype(o_ref.dtype)
    return xb


def _matmul_kernel(x_ref, w_ref, o_ref):
    _project(x_ref, w_ref, o_ref)


def _matmul(x, w, tm, out_dtype, name):
    t, k = x.shape
    n = w.shape[1]
    return pl.pallas_call(
        _matmul_kernel, grid=(t // tm,),
        in_specs=[pl.BlockSpec((tm, k), lambda i: (i, 0)), pl.BlockSpec((k, n), lambda i: (0, 0))],
        out_specs=pl.BlockSpec((tm, n), lambda i: (i, 0)),
        out_shape=jax.ShapeDtypeStruct((t, n), out_dtype),
        compiler_params=_params("parallel"), name=name)(x, w)


def _silu(x):
    h = 0.5 * x
    return h + h * jnp.tanh(h)


def _gelu(x):
    return 0.5 * x * (1.0 + lax.erf(x * (1.0 / math.sqrt(2.0))))


def _in_proj_kernel(x_ref, w_ref, wdt_ref, o_ref, dt_ref):
    xb = _project(x_ref, w_ref, o_ref)
    dt_ref[...] = _dot(xb, wdt_ref[...])


def _in_proj(x, w, w_dt):
    t, k = x.shape
    n = w.shape[1]
    tm = IN_PROJ_TM
    return pl.pallas_call(
        _in_proj_kernel, grid=(t // tm,),
        in_specs=[pl.BlockSpec((tm, k), lambda i: (i, 0)), pl.BlockSpec((k, n), lambda i: (0, 0)),
                  pl.BlockSpec((k, DT_PAD), lambda i: (0, 0))],
        out_specs=[pl.BlockSpec((tm, n), lambda i: (i, 0)), pl.BlockSpec((tm, DT_PAD), lambda i: (i, 0))],
        out_shape=[jax.ShapeDtypeStruct((t, n), BF16), jax.ShapeDtypeStruct((t, DT_PAD), F32)],
        compiler_params=_params("parallel"), name="in_proj")(x, w, w_dt)


def _proj_ln_kernel(*refs, n_in):
    x_refs, w_refs = refs[:n_in], refs[n_in:2 * n_in]
    h_ref, g_ref, b_ref, o_ref = refs[2 * n_in:]
    acc = _dot(jnp.concatenate([x_ref[...] for x_ref in x_refs], axis=1),
               jnp.concatenate([w_ref[...] for w_ref in w_refs], axis=0))
    o_ref[...] = _layer_norm(DEEPNORM_ALPHA * h_ref[...] + acc, g_ref[...], b_ref[...])


def _proj_ln(xs, ws, h, g, b, name):
    t, d = h.shape
    tm = LN_TM
    n_in = len(xs)
    in_specs = ([pl.BlockSpec((tm, x.shape[1]), lambda i: (i, 0)) for x in xs]
                + [pl.BlockSpec(w.shape, lambda i: (0, 0)) for w in ws]
                + [pl.BlockSpec((tm, d), lambda i: (i, 0)),
                   pl.BlockSpec((1, d), lambda i: (0, 0)), pl.BlockSpec((1, d), lambda i: (0, 0))])
    return pl.pallas_call(
        functools.partial(_proj_ln_kernel, n_in=n_in), grid=(t // tm,), in_specs=in_specs,
        out_specs=pl.BlockSpec((tm, d), lambda i: (i, 0)),
        out_shape=jax.ShapeDtypeStruct((t, d), F32),
        compiler_params=_params("parallel"), name=name)(*xs, *ws, h, g, b)


def _mlp_ln_kernel(h_ref, wu_ref, wd_ref, g_ref, b_ref, o_ref):
    h = h_ref[...]
    hb = h.astype(BF16)
    acc = jnp.zeros(h.shape, F32)
    for c in range(D_FF // MLP_FF_CHUNK):
        cols = slice(c * MLP_FF_CHUNK, (c + 1) * MLP_FF_CHUNK)
        u = jnp.maximum(_dot(hb, wu_ref[:, cols]), 0.0)
        acc += _dot((u * u).astype(BF16), wd_ref[cols, :])
    o_ref[...] = _layer_norm(DEEPNORM_ALPHA * h + acc, g_ref[...], b_ref[...])


def _mlp_ln(h, w_up, w_down, g, b, name):
    t, d = h.shape
    tm = MLP_TM
    return pl.pallas_call(
        _mlp_ln_kernel, grid=(t // tm,),
        in_specs=[pl.BlockSpec((tm, d), lambda i: (i, 0)),
                  pl.BlockSpec(w_up.shape, lambda i: (0, 0), pipeline_mode=pl.Buffered(1)),
                  pl.BlockSpec(w_down.shape, lambda i: (0, 0), pipeline_mode=pl.Buffered(1)),
                  pl.BlockSpec((1, d), lambda i: (0, 0)), pl.BlockSpec((1, d), lambda i: (0, 0))],
        out_specs=pl.BlockSpec((tm, d), lambda i: (i, 0)),
        out_shape=jax.ShapeDtypeStruct((t, d), F32),
        compiler_params=_params("parallel"), name=name)(h, w_up, w_down, g, b)


def _softplus(x):
    return jnp.maximum(x, 0.0) + jnp.log1p(jnp.exp(-jnp.abs(x)))


def _cumsum_rows(x, segment):
    row = lax.broadcasted_iota(jnp.int32, x.shape, 0) % segment
    shift = 1
    while shift < segment:
        x = x + jnp.where(row >= shift, pltpu.roll(x, shift, 0), 0.0)
        shift *= 2
    return x


def _ssd_kernel(z_ref, xbc_ref, dtr_ref, shift_ref, cw_ref, cb_ref, dtb_ref, alog_ref, dskip_ref, nw_ref, e_ref,
                o_ref, tail_ref, state_ref):
    L, C = SSD_L, SSD_CHUNK
    sub = V7X_SUBLANES

    @pl.when(pl.program_id(1) == 0)
    def _():
        tail_ref[...] = jnp.zeros(tail_ref.shape, F32)
        state_ref[...] = jnp.zeros(state_ref.shape, F32)

    x_b = xbc_ref[...]
    x_f = x_b.astype(F32)
    conv = cb_ref[...] + cw_ref[SSD_CONV - 1:SSD_CONV, :] * x_f
    tail = tail_ref[...]
    row8 = lax.broadcasted_iota(jnp.int32, tail.shape, 0)
    head_fix = jnp.zeros(tail.shape, F32)
    for d in range(1, SSD_CONV):
        w = cw_ref[SSD_CONV - 1 - d:SSD_CONV - d, :]
        conv += w * _dot(shift_ref[d - 1], x_b)
        head_fix += w * jnp.where(row8 < d, pltpu.roll(tail, d, 0), 0.0)
    conv = jnp.concatenate([conv[:sub] + head_fix, conv[sub:]], axis=0)
    tail_ref[...] = x_f[L - sub:, :]
    xbc = _silu(conv)
    xs = xbc[:, :SSD_INNER]
    b_b = xbc[:, SSD_INNER:SSD_INNER + SSD_BC].astype(BF16)
    c_b = xbc[:, SSD_INNER + SSD_BC:].astype(BF16)

    dt = _softplus(dtr_ref[...] + dtb_ref[...])
    a_cs = _cumsum_rows(dt * (-LOG2_E * jnp.exp(alog_ref[...])), C)
    a_end = jnp.concatenate([jnp.broadcast_to(a_cs[(c + 1) * C - 1:(c + 1) * C, :], (C, DT_PAD))
                             for c in range(L // C)], axis=0)
    exp_a = jnp.exp2(a_cs)
    to_end = jnp.exp2(a_end - a_cs)
    stacked = jnp.concatenate([dt, exp_a, dt * to_end], axis=0)
    hi = stacked.astype(BF16)
    lo = (stacked - hi.astype(F32)).astype(BF16)
    expanded = _dot(jnp.concatenate([hi, lo], axis=1), e_ref[...])
    dt_e, exp_a_e, w_end_e = expanded[:L], expanded[L:2 * L], expanded[2 * L:]

    xdt_b = (xs * dt_e).astype(BF16)
    xend_b = (xs * w_end_e).astype(BF16)
    causal = (lax.broadcasted_iota(jnp.int32, (C, C), 0) >= lax.broadcasted_iota(jnp.int32, (C, C), 1))
    lane = lax.broadcasted_iota(jnp.int32, (C, V7X_LANES), 1)
    heads_per_group = SSD_HEADS // SSD_GROUPS

    y_chunks = []
    for c in range(L // C):
        rows = slice(c * C, (c + 1) * C)
        a_c = a_cs[rows]
        a_c_t = a_c.T
        y_parts = []
        for g in range(SSD_GROUPS):
            gs = slice(g * SSD_STATE, (g + 1) * SSD_STATE)
            gw = slice(g * SSD_GROUP_WIDTH, (g + 1) * SSD_GROUP_WIDTH)
            cb = _dot_nt(c_b[rows, gs], b_b[rows, gs])
            y_off = _dot(c_b[rows, gs], state_ref[g].astype(BF16)) * exp_a_e[rows, gw]
            for pair in range(heads_per_group // 2):
                h0 = g * heads_per_group + 2 * pair
                x_pair = xdt_b[rows, h0 * SSD_HEAD_DIM:(h0 + 2) * SSD_HEAD_DIM]
                zero = jnp.zeros_like(x_pair)
                mixes = []
                for h in (h0, h0 + 1):
                    seg = a_c[:, h:h + 1] - a_c_t[h:h + 1, :]
                    mixes.append((cb * jnp.exp2(jnp.where(causal, seg, -jnp.inf))).astype(BF16))
                rhs = jnp.concatenate([jnp.where(lane < SSD_HEAD_DIM, x_pair, zero),
                                       jnp.where(lane >= SSD_HEAD_DIM, x_pair, zero)], axis=0)
                y_pair = _dot(jnp.concatenate(mixes, axis=1), rhs)
                y_parts.append(y_pair + y_off[:, 2 * pair * SSD_HEAD_DIM:(2 * pair + 2) * SSD_HEAD_DIM])
            upd = lax.dot_general(b_b[rows, gs], xend_b[rows, gw], (((0,), (0,)), ((), ())),
                                  preferred_element_type=F32)
            state_ref[g] = state_ref[g] * exp_a_e[(c + 1) * C - 1:(c + 1) * C, gw] + upd
        y_chunks.append(jnp.concatenate(y_parts, axis=1))

    y = jnp.concatenate(y_chunks, axis=0) + xs * dskip_ref[...]
    y = y * _silu(z_ref[...].astype(F32))
    normed = []
    for g in range(SSD_GROUPS):
        yg = y[:, g * SSD_GROUP_WIDTH:(g + 1) * SSD_GROUP_WIDTH]
        normed.append(yg * lax.rsqrt(jnp.mean(yg * yg, axis=-1, keepdims=True) + LN_EPS))
    o_ref[...] = (jnp.concatenate(normed, axis=1) * nw_ref[...]).astype(o_ref.dtype)


def _ssd_mixer(proj, dt_raw, conv_w, conv_b, dt_bias, a_log, d_skip, norm_w, bsz, seq):
    steps = seq // SSD_L
    row = lambda b, s: b * steps + s
    pad16 = lambda v: jnp.pad(v.astype(F32), (0, DT_PAD - SSD_HEADS)).reshape(1, DT_PAD)
    head_of_channel = jnp.arange(SSD_INNER) // SSD_HEAD_DIM
    expand = (jnp.arange(DT_PAD)[:, None] == head_of_channel[None, :]).astype(BF16)
    expand = jnp.concatenate([expand, expand], axis=0)
    t_idx = jnp.arange(SSD_L)
    shifts = jnp.stack([(t_idx[:, None] - d == t_idx[None, :]) for d in range(1, SSD_CONV)]).astype(BF16)
    d_e = jnp.repeat(d_skip.astype(F32), SSD_HEAD_DIM).reshape(1, SSD_INNER)
    const = lambda shape: pl.BlockSpec(shape, lambda b, s: (0,) * len(shape))
    return pl.pallas_call(
        _ssd_kernel, grid=(bsz, steps),
        in_specs=[pl.BlockSpec((SSD_L, SSD_INNER), lambda b, s: (row(b, s), 0)),
                  pl.BlockSpec((SSD_L, SSD_CONV_DIM), lambda b, s: (row(b, s), 2)),
                  pl.BlockSpec((SSD_L, DT_PAD), lambda b, s: (row(b, s), 0)),
                  const((SSD_CONV - 1, SSD_L, SSD_L)),
                  const((SSD_CONV, SSD_CONV_DIM)), const((1, SSD_CONV_DIM)),
                  const((1, DT_PAD)), const((1, DT_PAD)),
                  const((1, SSD_INNER)), const((1, SSD_INNER)), const((2 * DT_PAD, SSD_INNER))],
        out_specs=pl.BlockSpec((SSD_L, SSD_INNER), lambda b, s: (row(b, s), 0)),
        out_shape=jax.ShapeDtypeStruct((bsz * seq, SSD_INNER), BF16),
        scratch_shapes=[pltpu.VMEM((V7X_SUBLANES, SSD_CONV_DIM), F32),
                        pltpu.VMEM((SSD_GROUPS, SSD_STATE, SSD_GROUP_WIDTH), F32)],
        compiler_params=_params("parallel", "arbitrary"), name="ssd_mixer",
    )(proj, proj, dt_raw, shifts, conv_w.astype(F32), conv_b.astype(F32).reshape(1, -1), pad16(dt_bias),
      pad16(a_log), d_e, norm_w.astype(F32).reshape(1, -1), expand)


def _sgu_kernel(u_ref, v_ref, g_ref, b_ref, w_ref, bs_ref, o_ref):
    v = _layer_norm(_gelu(v_ref[...].astype(F32)), g_ref[...], b_ref[...]).astype(BF16)
    r = lax.broadcasted_iota(jnp.int32, (SGU_BLOCK, SGU_BLOCK), 0) // CHUNK
    c = lax.broadcasted_iota(jnp.int32, (SGU_BLOCK, SGU_BLOCK), 1) // CHUNK
    for g in range(SGU_GROUPS):
        w = jnp.where(r >= c, w_ref[g], 0.0).astype(BF16)
        cols = slice(g * SGU_GROUP_DIM, (g + 1) * SGU_GROUP_DIM)
        for blk in range(SGU_TM // SGU_BLOCK):
            rows = slice(blk * SGU_BLOCK, (blk + 1) * SGU_BLOCK)
            mixed = _dot(w, v[rows, cols]) + bs_ref[g]
            o_ref[rows, cols] = (_gelu(u_ref[rows, cols].astype(F32)) * mixed).astype(o_ref.dtype)


def _sgu_mixer(proj, ln_g, ln_b, w_s, b_s):
    t = proj.shape[0]
    bias = jnp.broadcast_to(b_s.astype(F32)[:, :, None], (SGU_GROUPS, SGU_BLOCK, SGU_GROUP_DIM))
    return pl.pallas_call(
        _sgu_kernel, grid=(t // SGU_TM,),
        in_specs=[pl.BlockSpec((SGU_TM, SGU_WIDTH), lambda i: (i, 1)),
                  pl.BlockSpec((SGU_TM, SGU_WIDTH), lambda i: (i, 2)),
                  pl.BlockSpec((1, SGU_WIDTH), lambda i: (0, 0)), pl.BlockSpec((1, SGU_WIDTH), lambda i: (0, 0)),
                  pl.BlockSpec((SGU_GROUPS, SGU_BLOCK, SGU_BLOCK), lambda i: (0, 0, 0)),
                  pl.BlockSpec((SGU_GROUPS, SGU_BLOCK, SGU_GROUP_DIM), lambda i: (0, 0, 0))],
        out_specs=pl.BlockSpec((SGU_TM, SGU_WIDTH), lambda i: (i, 0)),
        out_shape=jax.ShapeDtypeStruct((t, SGU_WIDTH), BF16),
        compiler_params=_params("parallel"), name="sgu_mixer",
    )(proj, proj, ln_g.astype(F32).reshape(1, -1), ln_b.astype(F32).reshape(1, -1), w_s.astype(F32), bias)


def _attn_tasks(n_tiles):
    rows = [(i, i) for i in range(n_tiles)] + [(i, j) for i in range(n_tiles) for j in range(i)]
    return len(rows), jnp.asarray(rows, jnp.int32).reshape(-1)


def _attn_kernel(slopes_ref, tasks_ref, q_ref, k_ref, v_ref, lq1_ref, lk1_ref, lq2_ref, lk2_ref, sw_ref, o_ref,
                 vt_ref, dbias_ref, kfeat_ref, qq_ref, s0_ref, s1_ref, cmax0_ref, cmax1_ref, m_ref, acc_ref,
                 *, lambda_init, n_tasks, n_tiles):
    t = ATT_T
    s_refs, cmax_refs = (s0_ref, s1_ref), (cmax0_ref, cmax1_ref)
    head, b = pl.program_id(0), pl.program_id(1)
    slope = slopes_ref[head]
    field = lambda n, f: tasks_ref[ATT_TASK_FIELDS * n + f]

    @pl.when(b == 0)
    def _():
        feat = lax.broadcasted_iota(jnp.int32, (ATT_FEAT, 2 * t), 0)
        for i in range(n_tiles):
            qq_ref[i, 2 * DIFF_HEAD_DIM:, :] = jnp.where(feat < 3, 1.0, 0.0).astype(F32).astype(BF16)
        key = lax.broadcasted_iota(jnp.int32, (t, t), 0)
        qry = lax.broadcasted_iota(jnp.int32, (t, t), 1)
        dbias_ref[...] = jnp.where(key // CHUNK <= qry // CHUNK,
                                   (-2.0 * slope) * jnp.maximum(key - qry, 0).astype(F32), -jnp.inf)
        bias = slope * lax.broadcasted_iota(jnp.int32, (t, ATT_FEAT), 0).astype(F32)
        hi = bias.astype(BF16).astype(F32)
        mid = (bias - hi).astype(BF16).astype(F32)
        lo = bias - hi - mid
        col = lax.broadcasted_iota(jnp.int32, (t, ATT_FEAT), 1)
        kfeat_ref[...] = jnp.where(col == 0, hi, jnp.where(col == 1, mid, jnp.where(col == 2, lo, 0.0))).astype(BF16)

    lam = (jnp.exp(jnp.sum(lq1_ref[...] * lk1_ref[...], axis=-1, keepdims=True))
           - jnp.exp(jnp.sum(lq2_ref[...] * lk2_ref[...], axis=-1, keepdims=True)) + lambda_init)

    ones = jnp.ones((ATT_ONES_ROWS, t), BF16)
    for c in range(n_tiles):
        vt_ref[c, :DIFF_V_DIM, :] = v_ref[c * t:(c + 1) * t, :].astype(F32).T.astype(BF16)
        vt_ref[c, DIFF_V_DIM:, :] = ones
    for i in range(n_tiles):
        q_t = q_ref[i * t:(i + 1) * t, :].astype(F32).T
        feat = lax.broadcasted_iota(jnp.int32, q_t.shape, 0)
        qq_ref[i, :2 * DIFF_HEAD_DIM, :t] = jnp.where(feat < DIFF_HEAD_DIM, q_t, 0.0).astype(BF16)
        qq_ref[i, :2 * DIFF_HEAD_DIM, t:] = jnp.where(feat >= DIFF_HEAD_DIM, q_t, 0.0).astype(BF16)

    half = t // 2
    late_queries = (slice(half, t), slice(t + half, 2 * t))
    quadrant = lambda late_cols, fill: jnp.concatenate(
        [jnp.full((late_cols[0].shape[0], half), fill, F32), late_cols[0],
         jnp.full((late_cols[1].shape[0], half), fill, F32), late_cols[1]], axis=1)

    def scores(n, slot, diagonal):
        i, j = field(n, 0), field(n, 1)
        lhs = jnp.concatenate([k_ref[pl.ds(pl.multiple_of(j * t, t), t), :], kfeat_ref[...]], axis=1)
        if diagonal:
            dbias = dbias_ref[...]
            early = _dot(lhs[:half], qq_ref[i]) + jnp.concatenate([dbias[:half], dbias[:half]], axis=1)
            late = [_dot(lhs[half:], qq_ref[i, :, cols]) + dbias[half:, half:] for cols in late_queries]
            s = jnp.concatenate([early, quadrant(late, -jnp.inf)], axis=0)
        else:
            s = _dot(lhs, qq_ref[i])
        s_refs[slot][...] = s
        cmax_refs[slot][...] = jnp.max(s, axis=0, keepdims=True)

    def accumulate(n, slot, diagonal):
        i, j = field(n, 0), field(n, 1)
        if diagonal:
            m_new = cmax_refs[slot][...]
            p = jnp.exp2(s_refs[slot][...] - m_new).astype(BF16)
            late = [_dot(vt_ref[j, :, half:], p[half:, cols]) for cols in late_queries]
            acc_ref[i] = _dot(vt_ref[j, :, :half], p[:half]) + quadrant(late, 0.0)
        else:
            shift = slope * ((j - i) * t).astype(F32)
            m = m_ref[i]
            m_new = jnp.maximum(m, cmax_refs[slot][...] + shift)
            alpha = jnp.exp2(m - m_new)
            p = jnp.exp2(s_refs[slot][...] - (m_new - shift)).astype(BF16)
            acc_ref[i] = alpha * acc_ref[i] + _dot(vt_ref[j], p)
        m_ref[i] = m_new

    def stage(n, scores_diagonal, consume_diagonal):
        scores(n + 1, 1 - n % 2, scores_diagonal)
        accumulate(n, n % 2, consume_diagonal)

    def finish(i):
        o = acc_ref[i, :DIFF_V_DIM, :] * (1.0 / acc_ref[i, DIFF_V_DIM:DIFF_V_DIM + 1, :])
        o = o[:, :t] - lam * o[:, t:]
        o = o * lax.rsqrt(jnp.mean(o * o, axis=0, keepdims=True) + LN_EPS)
        o_ref[i * t:(i + 1) * t, :] = (o.T * sw_ref[...] * (1.0 - lambda_init)).astype(o_ref.dtype)

    def run_stages(first, stop, diagonal):
        if first < stop and first % 2 == 1:
            stage(first, diagonal, diagonal)
            first += 1
        rounds = (stop - first) // ATT_UNROLL

        def body(r, carry):
            n = first + ATT_UNROLL * r
            for u in range(ATT_UNROLL):
                scores(n + u + 1, (u + 1) % 2, diagonal)
                accumulate(n + u, u % 2, diagonal)
            return carry

        if rounds > 0:
            lax.fori_loop(0, rounds, body, 0)
        for n in range(first + ATT_UNROLL * rounds, stop):
            stage(n, diagonal, diagonal)

    scores(0, 0, diagonal=True)
    run_stages(0, n_tiles - 1, diagonal=True)
    if n_tasks > n_tiles:
        stage(n_tiles - 1, False, True)
        run_stages(n_tiles, n_tasks - 1, diagonal=False)
    accumulate(n_tasks - 1, (n_tasks - 1) % 2, diagonal=n_tasks == n_tiles)
    for i in range(n_tiles):
        finish(i)


def _diff_attention(qkv, lq1, lk1, lq2, lk2, subln_w, lambda_init, bsz, seq):
    slopes = LOG2_E * jnp.exp2(-8.0 * jnp.arange(1, DIFF_HEADS + 1, dtype=F32) / DIFF_HEADS)
    n_tiles = seq // ATT_T
    n_tasks, tasks = _attn_tasks(n_tiles)
    vec = lambda p: p.astype(F32).reshape(1, -1)
    head_blk = 2 * DIFF_HEAD_DIM
    small = lambda n: pl.BlockSpec((1, n), lambda h, b: (0, 0))
    smem = pl.BlockSpec(memory_space=pltpu.SMEM)
    return pl.pallas_call(
        functools.partial(_attn_kernel, lambda_init=lambda_init, n_tasks=n_tasks, n_tiles=n_tiles),
        grid=(DIFF_HEADS, bsz),
        in_specs=[smem, smem,
                  pl.BlockSpec((None, seq, head_blk), lambda h, b: (b, 0, h)),
                  pl.BlockSpec((None, seq, head_blk), lambda h, b: (b, 0, DIFF_HEADS + h)),
                  pl.BlockSpec((None, seq, DIFF_V_DIM), lambda h, b: (b, 0, 2 * DIFF_HEADS + h)),
                  small(DIFF_HEAD_DIM), small(DIFF_HEAD_DIM), small(DIFF_HEAD_DIM), small(DIFF_HEAD_DIM),
                  small(DIFF_V_DIM)],
        out_specs=pl.BlockSpec((None, seq, DIFF_V_DIM), lambda h, b: (b, 0, h)),
        out_shape=jax.ShapeDtypeStruct((bsz, seq, DIFF_HEADS * DIFF_V_DIM), BF16),
        scratch_shapes=[pltpu.VMEM((n_tiles, DIFF_V_DIM + ATT_ONES_ROWS, ATT_T), BF16),
                        pltpu.VMEM((ATT_T, ATT_T), F32),
                        pltpu.VMEM((ATT_T, ATT_FEAT), BF16),
                        pltpu.VMEM((n_tiles, DIFF_V_DIM + ATT_FEAT, 2 * ATT_T), BF16),
                        pltpu.VMEM((ATT_T, 2 * ATT_T), F32), pltpu.VMEM((ATT_T, 2 * ATT_T), F32),
                        pltpu.VMEM((1, 2 * ATT_T), F32), pltpu.VMEM((1, 2 * ATT_T), F32),
                        pltpu.VMEM((n_tiles, 1, 2 * ATT_T), F32),
                        pltpu.VMEM((n_tiles, DIFF_V_DIM + ATT_ONES_ROWS, 2 * ATT_T), F32)],
        compiler_params=_params("arbitrary", "arbitrary"), name="diff_attention",
    )(slopes, tasks, qkv, qkv, qkv, vec(lq1), vec(lk1), vec(lq2), vec(lk2), vec(subln_w))


def kernel(x, even_w_in, even_conv_w, even_conv_b, even_dt_bias, even_a_log, even_d, even_ssd_norm_w, even_sgu_ln_g, even_sgu_ln_b, even_sgu_w, even_sgu_b, even_w_out, odd_w_qkv, odd_lambda_q1, odd_lambda_k1, odd_lambda_q2, odd_lambda_k2, odd_subln_w, odd_w_out, ln_mix_g, ln_mix_b, ln_ffn_g, ln_ffn_b, mlp_w_up, mlp_w_down):
    bsz, seq, d = x.shape
    h = x.reshape(bsz * seq, d)
    row = lambda p: p.astype(F32).reshape(1, -1)
    for l in range(DEPTH):
        i = l // 2
        if l % 2 == 0:
            w_in = even_w_in[i]
            o_z, o_xbc, o_dt, o_sgu = 0, SSD_INNER, SSD_INNER + SSD_CONV_DIM, SSD_INNER + SSD_CONV_DIM + SSD_HEADS
            w_main = jnp.concatenate([w_in[:, o_z:o_xbc], w_in[:, o_sgu:], w_in[:, o_xbc:o_dt]], axis=1).astype(BF16)
            w_dt = jnp.pad(w_in[:, o_dt:o_sgu], ((0, 0), (0, DT_PAD - SSD_HEADS))).astype(BF16)
            proj, dt_raw = _in_proj(h, w_main, w_dt)
            y_a = _ssd_mixer(proj, dt_raw, even_conv_w[i], even_conv_b[i], even_dt_bias[i], even_a_log[i],
                             even_d[i], even_ssd_norm_w[i], bsz, seq)
            y_b = _sgu_mixer(proj, even_sgu_ln_g[i], even_sgu_ln_b[i], even_sgu_w[i], even_sgu_b[i])
            w_out = even_w_out[i].astype(BF16)
            h = _proj_ln([y_a, y_b], [w_out[:SSD_INNER], w_out[SSD_INNER:]], h, row(ln_mix_g[l]), row(ln_mix_b[l]),
                         "even_out_proj_ln")
        else:
            lambda_init = 0.8 - 0.6 * math.exp(-0.3 * l)
            w_qkv = odd_w_qkv[i]
            q_scale = DIFF_HEAD_DIM ** -0.5 * LOG2_E
            w_qkv = jnp.concatenate([w_qkv[:, :DIFF_QK] * q_scale, w_qkv[:, DIFF_QK:]], axis=1).astype(BF16)
            qkv = _matmul(h, w_qkv, PROJ_TM, BF16, "qkv_proj").reshape(bsz, seq, -1)
            o = _diff_attention(qkv, odd_lambda_q1[i], odd_lambda_k1[i], odd_lambda_q2[i], odd_lambda_k2[i],
                                odd_subln_w[i], lambda_init, bsz, seq)
            h = _proj_ln([o.reshape(bsz * seq, -1)], [odd_w_out[i].astype(BF16)], h, row(ln_mix_g[l]), row(ln_mix_b[l]),
                         "odd_out_proj_ln")
        h = _mlp_ln(h, mlp_w_up[l].astype(BF16), mlp_w_down[l].astype(BF16), row(ln_ffn_g[l]), row(ln_ffn_b[l]),
                    "mlp_ln_%d" % l)
    return h.reshape(bsz, seq, d)
```

```python
import functools
import math

import jax
import jax.numpy as jnp
from jax import lax
from jax.experimental import pallas as pl
from jax.experimental.pallas import tpu as pltpu

F32 = jnp.float32
BF16 = jnp.bfloat16

D_MODEL = 1024
DEPTH = 2
CHUNK = 64
SSD_HEADS = 16
SSD_HEAD_DIM = 64
SSD_INNER = SSD_HEADS * SSD_HEAD_DIM
SSD_GROUPS = 2
SSD_STATE = 128
SSD_CONV = 4
SSD_BC = SSD_GROUPS * SSD_STATE
SSD_CONV_DIM = SSD_INNER + 2 * SSD_BC
SSD_GROUP_WIDTH = SSD_INNER // SSD_GROUPS
SGU_BLOCK = 128
SGU_GROUPS = 8
SGU_WIDTH = 1024
SGU_GROUP_DIM = SGU_WIDTH // SGU_GROUPS
DIFF_HEADS = 8
DIFF_HEAD_DIM = 64
DIFF_V_DIM = 2 * DIFF_HEAD_DIM
DIFF_QK = DIFF_HEADS * 2 * DIFF_HEAD_DIM
D_FF = 4 * D_MODEL
DEEPNORM_ALPHA = (2 * DEPTH) ** 0.25
LN_EPS = 1e-5

V7X_LANES = 128
V7X_SUBLANES = 8
V7X_VMEM_BYTES = 64 * 1024 * 1024
VMEM_LIMIT = V7X_VMEM_BYTES * 7 // 8

PROJ_TM = 1024
IN_PROJ_TM = 1024
PROJ_TN = 1536
LN_TM = 1024
MLP_TM = 1024
MLP_FF_CHUNK = 1024
SSD_L = 256
SSD_CHUNK = 128
SGU_TM = 1024
ATT_T = 512
ATT_ONES_ROWS = 16
ATT_FEAT = V7X_LANES
ATT_TASK_FIELDS = 2
ATT_UNROLL = 8
LOG2_E = 1.0 / math.log(2.0)
DT_PAD = V7X_LANES


def _params(*sem):
    return pltpu.CompilerParams(dimension_semantics=sem, vmem_limit_bytes=VMEM_LIMIT)


def _layer_norm(y, g, b):
    mu = jnp.mean(y, axis=-1, keepdims=True)
    yc = y - mu
    var = jnp.mean(yc * yc, axis=-1, keepdims=True)
    return yc * lax.rsqrt(var + LN_EPS) * g + b


def _dot(a, b):
    return jnp.dot(a, b, preferred_element_type=F32)


def _dot_nt(a, b):
    return lax.dot_general(a, b, (((1,), (1,)), ((), ())), preferred_element_type=F32)


def _project(x_ref, w_ref, o_ref):
    xb = x_ref[...].astype(BF16)
    for c in range(w_ref.shape[1] // PROJ_TN):
        cols = slice(c * PROJ_TN, (c + 1) * PROJ_TN)
        o_ref[:, cols] = _dot(xb, w_ref[:, cols]).astype(o_ref.dtype)
    return xb


def _matmul_kernel(x_ref, w_ref, o_ref):
    _project(x_ref, w_ref, o_ref)


def _matmul(x, w, tm, out_dtype, name):
    t, k = x.shape
    n = w.shape[1]
    return pl.pallas_call(
        _matmul_kernel, grid=(t // tm,),
        in_specs=[pl.BlockSpec((tm, k), lambda i: (i, 0)), pl.BlockSpec((k, n), lambda i: (0, 0))],
        out_specs=pl.BlockSpec((tm, n), lambda i: (i, 0)),
        out_shape=jax.ShapeDtypeStruct((t, n), out_dtype),
        compiler_params=_params("parallel"), name=name)(x, w)


def _silu(x):
    h = 0.5 * x
    return h + h * jnp.tanh(h)


def _gelu(x):
    return 0.5 * x * (1.0 + lax.erf(x * (1.0 / math.sqrt(2.0))))


def _in_proj_kernel(x_ref, w_ref, wdt_ref, o_ref, dt_ref):
    xb = _project(x_ref, w_ref, o_ref)
    dt_ref[...] = _dot(xb, wdt_ref[...])


def _in_proj(x, w, w_dt):
    t, k = x.shape
    n = w.shape[1]
    tm = IN_PROJ_TM
    return pl.pallas_call(
        _in_proj_kernel, grid=(t // tm,),
        in_specs=[pl.BlockSpec((tm, k), lambda i: (i, 0)),
                  pl.BlockSpec((k, n), lambda i: (0, 0), pipeline_mode=pl.Buffered(1)),
                  pl.BlockSpec((k, DT_PAD), lambda i: (0, 0))],
        out_specs=[pl.BlockSpec((tm, n), lambda i: (i, 0)), pl.BlockSpec((tm, DT_PAD), lambda i: (i, 0))],
        out_shape=[jax.ShapeDtypeStruct((t, n), BF16), jax.ShapeDtypeStruct((t, DT_PAD), F32)],
        compiler_params=_params("parallel"), name="in_proj")(x, w, w_dt)


def _proj_ln_kernel(*refs, n_in):
    x_refs, w_refs = refs[:n_in], refs[n_in:2 * n_in]
    h_ref, g_ref, b_ref, o_ref = refs[2 * n_in:]
    acc = _dot(jnp.concatenate([x_ref[...] for x_ref in x_refs], axis=1),
               jnp.concatenate([w_ref[...] for w_ref in w_refs], axis=0))
    o_ref[...] = _layer_norm(DEEPNORM_ALPHA * h_ref[...] + acc, g_ref[...], b_ref[...])


def _proj_ln(xs, ws, h, g, b, name):
    t, d = h.shape
    tm = LN_TM
    n_in = len(xs)
    in_specs = ([pl.BlockSpec((tm, x.shape[1]), lambda i: (i, 0)) for x in xs]
                + [pl.BlockSpec(w.shape, lambda i: (0, 0)) for w in ws]
                + [pl.BlockSpec((tm, d), lambda i: (i, 0)),
                   pl.BlockSpec((1, d), lambda i: (0, 0)), pl.BlockSpec((1, d), lambda i: (0, 0))])
    return pl.pallas_call(
        functools.partial(_proj_ln_kernel, n_in=n_in), grid=(t // tm,), in_specs=in_specs,
        out_specs=pl.BlockSpec((tm, d), lambda i: (i, 0)),
        out_shape=jax.ShapeDtypeStruct((t, d), F32),
        compiler_params=_params("parallel"), name=name)(*xs, *ws, h, g, b)


def _mlp_ln_kernel(h_ref, wu_ref, wd_ref, g_ref, b_ref, o_ref):
    h = h_ref[...]
    hb = h.astype(BF16)
    acc = jnp.zeros(h.shape, F32)
    for c in range(D_FF // MLP_FF_CHUNK):
        cols = slice(c * MLP_FF_CHUNK, (c + 1) * MLP_FF_CHUNK)
        u = jnp.maximum(_dot(hb, wu_ref[:, cols]), 0.0)
        acc += _dot((u * u).astype(BF16), wd_ref[cols, :])
    o_ref[...] = _layer_norm(DEEPNORM_ALPHA * h + acc, g_ref[...], b_ref[...])


def _mlp_ln(h, w_up, w_down, g, b, name):
    t, d = h.shape
    tm = MLP_TM
    return pl.pallas_call(
        _mlp_ln_kernel, grid=(t // tm,),
        in_specs=[pl.BlockSpec((tm, d), lambda i: (i, 0)),
                  pl.BlockSpec(w_up.shape, lambda i: (0, 0), pipeline_mode=pl.Buffered(1)),
                  pl.BlockSpec(w_down.shape, lambda i: (0, 0), pipeline_mode=pl.Buffered(1)),
                  pl.BlockSpec((1, d), lambda i: (0, 0)), pl.BlockSpec((1, d), lambda i: (0, 0))],
        out_specs=pl.BlockSpec((tm, d), lambda i: (i, 0)),
        out_shape=jax.ShapeDtypeStruct((t, d), F32),
        compiler_params=_params("parallel"), name=name)(h, w_up, w_down, g, b)


def _softplus(x):
    return jnp.maximum(x, 0.0) + jnp.log1p(jnp.exp(-jnp.abs(x)))


def _cumsum_rows(x, segment):
    row = lax.broadcasted_iota(jnp.int32, x.shape, 0) % segment
    shift = 1
    while shift < segment:
        x = x + jnp.where(row >= shift, pltpu.roll(x, shift, 0), 0.0)
        shift *= 2
    return x


def _ssd_kernel(z_ref, xbc_ref, dtr_ref, shift_ref, cw_ref, cb_ref, dtb_ref, alog_ref, dskip_ref, nw_ref, e_ref,
                o_ref, tail_ref, state_ref):
    L, C = SSD_L, SSD_CHUNK
    sub = V7X_SUBLANES

    @pl.when(pl.program_id(1) == 0)
    def _():
        tail_ref[...] = jnp.zeros(tail_ref.shape, F32)
        state_ref[...] = jnp.zeros(state_ref.shape, F32)

    x_b = xbc_ref[...]
    x_f = x_b.astype(F32)
    conv = cb_ref[...] + cw_ref[SSD_CONV - 1:SSD_CONV, :] * x_f
    tail = tail_ref[...]
    row8 = lax.broadcasted_iota(jnp.int32, tail.shape, 0)
    head_fix = jnp.zeros(tail.shape, F32)
    for d in range(1, SSD_CONV):
        w = cw_ref[SSD_CONV - 1 - d:SSD_CONV - d, :]
        conv += w * _dot(shift_ref[d - 1], x_b)
        head_fix += w * jnp.where(row8 < d, pltpu.roll(tail, d, 0), 0.0)
    conv = jnp.concatenate([conv[:sub] + head_fix, conv[sub:]], axis=0)
    tail_ref[...] = x_f[L - sub:, :]
    xbc = _silu(conv)
    xs = xbc[:, :SSD_INNER]
    b_b = xbc[:, SSD_INNER:SSD_INNER + SSD_BC].astype(BF16)
    c_b = xbc[:, SSD_INNER + SSD_BC:].astype(BF16)

    dt = _softplus(dtr_ref[...] + dtb_ref[...])
    a_cs = _cumsum_rows(dt * (-LOG2_E * jnp.exp(alog_ref[...])), C)
    a_end = jnp.concatenate([jnp.broadcast_to(a_cs[(c + 1) * C - 1:(c + 1) * C, :], (C, DT_PAD))
                             for c in range(L // C)], axis=0)
    exp_a = jnp.exp2(a_cs)
    to_end = jnp.exp2(a_end - a_cs)
    stacked = jnp.concatenate([dt, exp_a, dt * to_end], axis=0)
    hi = stacked.astype(BF16)
    lo = (stacked - hi.astype(F32)).astype(BF16)
    expanded = _dot(jnp.concatenate([hi, lo], axis=1), e_ref[...])
    dt_e, exp_a_e, w_end_e = expanded[:L], expanded[L:2 * L], expanded[2 * L:]

    xdt_b = (xs * dt_e).astype(BF16)
    xend_b = (xs * w_end_e).astype(BF16)
    causal = (lax.broadcasted_iota(jnp.int32, (C, C), 0) >= lax.broadcasted_iota(jnp.int32, (C, C), 1))
    lane = lax.broadcasted_iota(jnp.int32, (C, V7X_LANES), 1)
    heads_per_group = SSD_HEADS // SSD_GROUPS

    y_chunks = []
    for c in range(L // C):
        rows = slice(c * C, (c + 1) * C)
        a_c = a_cs[rows]
        a_c_t = a_c.T
        y_parts = []
        for g in range(SSD_GROUPS):
            gs = slice(g * SSD_STATE, (g + 1) * SSD_STATE)
            gw = slice(g * SSD_GROUP_WIDTH, (g + 1) * SSD_GROUP_WIDTH)
            cb = _dot_nt(c_b[rows, gs], b_b[rows, gs])
            y_off = _dot(c_b[rows, gs], state_ref[g].astype(BF16)) * exp_a_e[rows, gw]
            for pair in range(heads_per_group // 2):
                h0 = g * heads_per_group + 2 * pair
                x_pair = xdt_b[rows, h0 * SSD_HEAD_DIM:(h0 + 2) * SSD_HEAD_DIM]
                zero = jnp.zeros_like(x_pair)
                mixes = []
                for h in (h0, h0 + 1):
                    seg = a_c[:, h:h + 1] - a_c_t[h:h + 1, :]
                    mixes.append((cb * jnp.exp2(jnp.where(causal, seg, -jnp.inf))).astype(BF16))
                rhs = jnp.concatenate([jnp.where(lane < SSD_HEAD_DIM, x_pair, zero),
                                       jnp.where(lane >= SSD_HEAD_DIM, x_pair, zero)], axis=0)
                y_pair = _dot(jnp.concatenate(mixes, axis=1), rhs)
                y_parts.append(y_pair + y_off[:, 2 * pair * SSD_HEAD_DIM:(2 * pair + 2) * SSD_HEAD_DIM])
            upd = lax.dot_general(b_b[rows, gs], xend_b[rows, gw], (((0,), (0,)), ((), ())),
                                  preferred_element_type=F32)
            state_ref[g] = state_ref[g] * exp_a_e[(c + 1) * C - 1:(c + 1) * C, gw] + upd
        y_chunks.append(jnp.concatenate(y_parts, axis=1))

    y = jnp.concatenate(y_chunks, axis=0) + xs * dskip_ref[...]
    y = y * _silu(z_ref[...].astype(F32))
    normed = []
    for g in range(SSD_GROUPS):
        yg = y[:, g * SSD_GROUP_WIDTH:(g + 1) * SSD_GROUP_WIDTH]
        normed.append(yg * lax.rsqrt(jnp.mean(yg * yg, axis=-1, keepdims=True) + LN_EPS))
    o_ref[...] = (jnp.concatenate(normed, axis=1) * nw_ref[...]).astype(o_ref.dtype)


def _ssd_mixer(proj, dt_raw, conv_w, conv_b, dt_bias, a_log, d_skip, norm_w, bsz, seq):
    steps = seq // SSD_L
    row = lambda b, s: b * steps + s
    pad16 = lambda v: jnp.pad(v.astype(F32), (0, DT_PAD - SSD_HEADS)).reshape(1, DT_PAD)
    head_of_channel = jnp.arange(SSD_INNER) // SSD_HEAD_DIM
    expand = (jnp.arange(DT_PAD)[:, None] == head_of_channel[None, :]).astype(BF16)
    expand = jnp.concatenate([expand, expand], axis=0)
    t_idx = jnp.arange(SSD_L)
    shifts = jnp.stack([(t_idx[:, None] - d == t_idx[None, :]) for d in range(1, SSD_CONV)]).astype(BF16)
    d_e = jnp.repeat(d_skip.astype(F32), SSD_HEAD_DIM).reshape(1, SSD_INNER)
    const = lambda shape: pl.BlockSpec(shape, lambda b, s: (0,) * len(shape))
    return pl.pallas_call(
        _ssd_kernel, grid=(bsz, steps),
        in_specs=[pl.BlockSpec((SSD_L, SSD_INNER), lambda b, s: (row(b, s), 0)),
                  pl.BlockSpec((SSD_L, SSD_CONV_DIM), lambda b, s: (row(b, s), 2)),
                  pl.BlockSpec((SSD_L, DT_PAD), lambda b, s: (row(b, s), 0)),
                  const((SSD_CONV - 1, SSD_L, SSD_L)),
                  const((SSD_CONV, SSD_CONV_DIM)), const((1, SSD_CONV_DIM)),
                  const((1, DT_PAD)), const((1, DT_PAD)),
                  const((1, SSD_INNER)), const((1, SSD_INNER)), const((2 * DT_PAD, SSD_INNER))],
        out_specs=pl.BlockSpec((SSD_L, SSD_INNER), lambda b, s: (row(b, s), 0)),
        out_shape=jax.ShapeDtypeStruct((bsz * seq, SSD_INNER), BF16),
        scratch_shapes=[pltpu.VMEM((V7X_SUBLANES, SSD_CONV_DIM), F32),
                        pltpu.VMEM((SSD_GROUPS, SSD_STATE, SSD_GROUP_WIDTH), F32)],
        compiler_params=_params("parallel", "arbitrary"), name="ssd_mixer",
    )(proj, proj, dt_raw, shifts, conv_w.astype(F32), conv_b.astype(F32).reshape(1, -1), pad16(dt_bias),
      pad16(a_log), d_e, norm_w.astype(F32).reshape(1, -1), expand)


def _sgu_kernel(u_ref, v_ref, g_ref, b_ref, w_ref, bs_ref, o_ref):
    v = _layer_norm(_gelu(v_ref[...].astype(F32)), g_ref[...], b_ref[...]).astype(BF16)
    r = lax.broadcasted_iota(jnp.int32, (SGU_BLOCK, SGU_BLOCK), 0) // CHUNK
    c = lax.broadcasted_iota(jnp.int32, (SGU_BLOCK, SGU_BLOCK), 1) // CHUNK
    for g in range(SGU_GROUPS):
        w = jnp.where(r >= c, w_ref[g], 0.0).astype(BF16)
        cols = slice(g * SGU_GROUP_DIM, (g + 1) * SGU_GROUP_DIM)
        for blk in range(SGU_TM // SGU_BLOCK):
            rows = slice(blk * SGU_BLOCK, (blk + 1) * SGU_BLOCK)
            mixed = _dot(w, v[rows, cols]) + bs_ref[g]
            o_ref[rows, cols] = (_gelu(u_ref[rows, cols].astype(F32)) * mixed).astype(o_ref.dtype)


def _sgu_mixer(proj, ln_g, ln_b, w_s, b_s):
    t = proj.shape[0]
    bias = jnp.broadcast_to(b_s.astype(F32)[:, :, None], (SGU_GROUPS, SGU_BLOCK, SGU_GROUP_DIM))
    return pl.pallas_call(
        _sgu_kernel, grid=(t // SGU_TM,),
        in_specs=[pl.BlockSpec((SGU_TM, SGU_WIDTH), lambda i: (i, 1)),
                  pl.BlockSpec((SGU_TM, SGU_WIDTH), lambda i: (i, 2)),
                  pl.BlockSpec((1, SGU_WIDTH), lambda i: (0, 0)), pl.BlockSpec((1, SGU_WIDTH), lambda i: (0, 0)),
                  pl.BlockSpec((SGU_GROUPS, SGU_BLOCK, SGU_BLOCK), lambda i: (0, 0, 0)),
                  pl.BlockSpec((SGU_GROUPS, SGU_BLOCK, SGU_GROUP_DIM), lambda i: (0, 0, 0))],
        out_specs=pl.BlockSpec((SGU_TM, SGU_WIDTH), lambda i: (i, 0)),
        out_shape=jax.ShapeDtypeStruct((t, SGU_WIDTH), BF16),
        compiler_params=_params("parallel"), name="sgu_mixer",
    )(proj, proj, ln_g.astype(F32).reshape(1, -1), ln_b.astype(F32).reshape(1, -1), w_s.astype(F32), bias)


def _attn_tasks(n_tiles):
    rows = [(i, i) for i in range(n_tiles)] + [(i, j) for i in range(n_tiles) for j in range(i)]
    return len(rows), jnp.asarray(rows, jnp.int32).reshape(-1)


def _attn_kernel(slopes_ref, tasks_ref, q_ref, k_ref, v_ref, lq1_ref, lk1_ref, lq2_ref, lk2_ref, sw_ref, o_ref,
                 vt_ref, dbias_ref, kfeat_ref, qq_ref, s0_ref, s1_ref, cmax0_ref, cmax1_ref, m_ref, acc_ref,
                 *, lambda_init, n_tasks, n_tiles):
    t = ATT_T
    s_refs, cmax_refs = (s0_ref, s1_ref), (cmax0_ref, cmax1_ref)
    head, b = pl.program_id(0), pl.program_id(1)
    slope = slopes_ref[head]
    field = lambda n, f: tasks_ref[ATT_TASK_FIELDS * n + f]

    @pl.when(b == 0)
    def _():
        feat = lax.broadcasted_iota(jnp.int32, (ATT_FEAT, 2 * t), 0)
        for i in range(n_tiles):
            qq_ref[i, 2 * DIFF_HEAD_DIM:, :] = jnp.where(feat < 3, 1.0, 0.0).astype(F32).astype(BF16)
        key = lax.broadcasted_iota(jnp.int32, (t, t), 0)
        qry = lax.broadcasted_iota(jnp.int32, (t, t), 1)
        dbias_ref[...] = jnp.where(key // CHUNK <= qry // CHUNK,
                                   (-2.0 * slope) * jnp.maximum(key - qry, 0).astype(F32), -jnp.inf)
        bias = slope * lax.broadcasted_iota(jnp.int32, (t, ATT_FEAT), 0).astype(F32)
        hi = bias.astype(BF16).astype(F32)
        mid = (bias - hi).astype(BF16).astype(F32)
        lo = bias - hi - mid
        col = lax.broadcasted_iota(jnp.int32, (t, ATT_FEAT), 1)
        kfeat_ref[...] = jnp.where(col == 0, hi, jnp.where(col == 1, mid, jnp.where(col == 2, lo, 0.0))).astype(BF16)

    lam = (jnp.exp(jnp.sum(lq1_ref[...] * lk1_ref[...], axis=-1, keepdims=True))
           - jnp.exp(jnp.sum(lq2_ref[...] * lk2_ref[...], axis=-1, keepdims=True)) + lambda_init)

    ones = jnp.ones((ATT_ONES_ROWS, t), BF16)
    for c in range(n_tiles):
        vt_ref[c, :DIFF_V_DIM, :] = v_ref[c * t:(c + 1) * t, :].astype(F32).T.astype(BF16)
        vt_ref[c, DIFF_V_DIM:, :] = ones
    for i in range(n_tiles):
        q_t = q_ref[i * t:(i + 1) * t, :].astype(F32).T
        feat = lax.broadcasted_iota(jnp.int32, q_t.shape, 0)
        qq_ref[i, :2 * DIFF_HEAD_DIM, :t] = jnp.where(feat < DIFF_HEAD_DIM, q_t, 0.0).astype(BF16)
        qq_ref[i, :2 * DIFF_HEAD_DIM, t:] = jnp.where(feat >= DIFF_HEAD_DIM, q_t, 0.0).astype(BF16)

    half = t // 2
    late_queries = (slice(half, t), slice(t + half, 2 * t))
    quadrant = lambda late_cols, fill: jnp.concatenate(
        [jnp.full((late_cols[0].shape[0], half), fill, F32), late_cols[0],
         jnp.full((late_cols[1].shape[0], half), fill, F32), late_cols[1]], axis=1)

    def scores(n, slot, diagonal):
        i, j = field(n, 0), field(n, 1)
        lhs = jnp.concatenate([k_ref[pl.ds(pl.multiple_of(j * t, t), t), :], kfeat_ref[...]], axis=1)
        if diagonal:
            dbias = dbias_ref[...]
            early = _dot(lhs[:half], qq_ref[i]) + jnp.concatenate([dbias[:half], dbias[:half]], axis=1)
            late = [_dot(lhs[half:], qq_ref[i, :, cols]) + dbias[half:, half:] for cols in late_queries]
            s = jnp.concatenate([early, quadrant(late, -jnp.inf)], axis=0)
        else:
            s = _dot(lhs, qq_ref[i])
        s_refs[slot][...] = s
        cmax_refs[slot][...] = jnp.max(s, axis=0, keepdims=True)

    def accumulate(n, slot, diagonal):
        i, j = field(n, 0), field(n, 1)
        if diagonal:
            m_new = cmax_refs[slot][...]
            p = jnp.exp2(s_refs[slot][...] - m_new).astype(BF16)
            late = [_dot(vt_ref[j, :, half:], p[half:, cols]) for cols in late_queries]
            acc_ref[i] = _dot(vt_ref[j, :, :half], p[:half]) + quadrant(late, 0.0)
        else:
            shift = slope * ((j - i) * t).astype(F32)
            m = m_ref[i]
            m_new = jnp.maximum(m, cmax_refs[slot][...] + shift)
            alpha = jnp.exp2(m - m_new)
            p = jnp.exp2(s_refs[slot][...] - (m_new - shift)).astype(BF16)
            acc_ref[i] = alpha * acc_ref[i] + _dot(vt_ref[j], p)
        m_ref[i] = m_new

    def stage(n, scores_diagonal, consume_diagonal):
        scores(n + 1, 1 - n % 2, scores_diagonal)
        accumulate(n, n % 2, consume_diagonal)

    def finish(i):
        o = acc_ref[i, :DIFF_V_DIM, :] * (1.0 / acc_ref[i, DIFF_V_DIM:DIFF_V_DIM + 1, :])
        o = o[:, :t] - lam * o[:, t:]
        o = o * lax.rsqrt(jnp.mean(o * o, axis=0, keepdims=True) + LN_EPS)
        o_ref[i * t:(i + 1) * t, :] = (o.T * sw_ref[...] * (1.0 - lambda_init)).astype(o_ref.dtype)

    def run_stages(first, stop, diagonal):
        if first < stop and first % 2 == 1:
            stage(first, diagonal, diagonal)
            first += 1
        rounds = (stop - first) // ATT_UNROLL

        def body(r, carry):
            n = first + ATT_UNROLL * r
            for u in range(ATT_UNROLL):
                scores(n + u + 1, (u + 1) % 2, diagonal)
                accumulate(n + u, u % 2, diagonal)
            return carry

        if rounds > 0:
            lax.fori_loop(0, rounds, body, 0)
        for n in range(first + ATT_UNROLL * rounds, stop):
            stage(n, diagonal, diagonal)

    scores(0, 0, diagonal=True)
    run_stages(0, n_tiles - 1, diagonal=True)
    if n_tasks > n_tiles:
        stage(n_tiles - 1, False, True)
        run_stages(n_tiles, n_tasks - 1, diagonal=False)
    accumulate(n_tasks - 1, (n_tasks - 1) % 2, diagonal=n_tasks == n_tiles)
    for i in range(n_tiles):
        finish(i)


def _diff_attention(qkv, lq1, lk1, lq2, lk2, subln_w, lambda_init, bsz, seq):
    slopes = LOG2_E * jnp.exp2(-8.0 * jnp.arange(1, DIFF_HEADS + 1, dtype=F32) / DIFF_HEADS)
    n_tiles = seq // ATT_T
    n_tasks, tasks = _attn_tasks(n_tiles)
    vec = lambda p: p.astype(F32).reshape(1, -1)
    head_blk = 2 * DIFF_HEAD_DIM
    small = lambda n: pl.BlockSpec((1, n), lambda h, b: (0, 0))
    smem = pl.BlockSpec(memory_space=pltpu.SMEM)
    return pl.pallas_call(
        functools.partial(_attn_kernel, lambda_init=lambda_init, n_tasks=n_tasks, n_tiles=n_tiles),
        grid=(DIFF_HEADS, bsz),
        in_specs=[smem, smem,
                  pl.BlockSpec((None, seq, head_blk), lambda h, b: (b, 0, h)),
                  pl.BlockSpec((None, seq, head_blk), lambda h, b: (b, 0, DIFF_HEADS + h)),
                  pl.BlockSpec((None, seq, DIFF_V_DIM), lambda h, b: (b, 0, 2 * DIFF_HEADS + h)),
                  small(DIFF_HEAD_DIM), small(DIFF_HEAD_DIM), small(DIFF_HEAD_DIM), small(DIFF_HEAD_DIM),
                  small(DIFF_V_DIM)],
        out_specs=pl.BlockSpec((None, seq, DIFF_V_DIM), lambda h, b: (b, 0, h)),
        out_shape=jax.ShapeDtypeStruct((bsz, seq, DIFF_HEADS * DIFF_V_DIM), BF16),
        scratch_shapes=[pltpu.VMEM((n_tiles, DIFF_V_DIM + ATT_ONES_ROWS, ATT_T), BF16),
                        pltpu.VMEM((ATT_T, ATT_T), F32),
                        pltpu.VMEM((ATT_T, ATT_FEAT), BF16),
                        pltpu.VMEM((n_tiles, DIFF_V_DIM + ATT_FEAT, 2 * ATT_T), BF16),
                        pltpu.VMEM((ATT_T, 2 * ATT_T), F32), pltpu.VMEM((ATT_T, 2 * ATT_T), F32),
                        pltpu.VMEM((1, 2 * ATT_T), F32), pltpu.VMEM((1, 2 * ATT_T), F32),
                        pltpu.VMEM((n_tiles, 1, 2 * ATT_T), F32),
                        pltpu.VMEM((n_tiles, DIFF_V_DIM + ATT_ONES_ROWS, 2 * ATT_T), F32)],
        compiler_params=_params("arbitrary", "arbitrary"), name="diff_attention",
    )(slopes, tasks, qkv, qkv, qkv, vec(lq1), vec(lk1), vec(lq2), vec(lk2), vec(subln_w))


def kernel(x, even_w_in, even_conv_w, even_conv_b, even_dt_bias, even_a_log, even_d, even_ssd_norm_w, even_sgu_ln_g, even_sgu_ln_b, even_sgu_w, even_sgu_b, even_w_out, odd_w_qkv, odd_lambda_q1, odd_lambda_k1, odd_lambda_q2, odd_lambda_k2, odd_subln_w, odd_w_out, ln_mix_g, ln_mix_b, ln_ffn_g, ln_ffn_b, mlp_w_up, mlp_w_down):
    bsz, seq, d = x.shape
    h = x.reshape(bsz * seq, d)
    row = lambda p: p.astype(F32).reshape(1, -1)
    for l in range(DEPTH):
        i = l // 2
        if l % 2 == 0:
            w_in = even_w_in[i]
            o_z, o_xbc, o_dt, o_sgu = 0, SSD_INNER, SSD_INNER + SSD_CONV_DIM, SSD_INNER + SSD_CONV_DIM + SSD_HEADS
            w_main = jnp.concatenate([w_in[:, o_z:o_xbc], w_in[:, o_sgu:], w_in[:, o_xbc:o_dt]], axis=1).astype(BF16)
            w_dt = jnp.pad(w_in[:, o_dt:o_sgu], ((0, 0), (0, DT_PAD - SSD_HEADS))).astype(BF16)
            proj, dt_raw = _in_proj(h, w_main, w_dt)
            y_a = _ssd_mixer(proj, dt_raw, even_conv_w[i], even_conv_b[i], even_dt_bias[i], even_a_log[i],
                             even_d[i], even_ssd_norm_w[i], bsz, seq)
            y_b = _sgu_mixer(proj, even_sgu_ln_g[i], even_sgu_ln_b[i], even_sgu_w[i], even_sgu_b[i])
            w_out = even_w_out[i].astype(BF16)
            h = _proj_ln([y_a, y_b], [w_out[:SSD_INNER], w_out[SSD_INNER:]], h, row(ln_mix_g[l]), row(ln_mix_b[l]),
                         "even_out_proj_ln")
        else:
            lambda_init = 0.8 - 0.6 * math.exp(-0.3 * l)
            w_qkv = odd_w_qkv[i]
            q_scale = DIFF_HEAD_DIM ** -0.5 * LOG2_E
            w_qkv = jnp.concatenate([w_qkv[:, :DIFF_QK] * q_scale, w_qkv[:, DIFF_QK:]], axis=1).astype(BF16)
            qkv = _matmul(h, w_qkv, PROJ_TM, BF16, "qkv_proj").reshape(bsz, seq, -1)
            o = _diff_attention(qkv, odd_lambda_q1[i], odd_lambda_k1[i], odd_lambda_q2[i], odd_lambda_k2[i],
                                odd_subln_w[i], lambda_init, bsz, seq)
            h = _proj_ln([o.reshape(bsz * seq, -1)], [odd_w_out[i].astype(BF16)], h, row(ln_mix_g[l]), row(ln_mix_b[l]),
                         "odd_out_proj_ln")
        h = _mlp_ln(h, mlp_w_up[l].astype(BF16), mlp_w_down[l].astype(BF16), row(ln_ffn_g[l]), row(ln_ffn_b[l]),
                    "mlp_ln_%d" % l)
    return h.reshape(bsz, seq, d)
```
